```python
import math
import jax, jax.numpy as jnp
from jax import lax
import numpy as np

D_MODEL = 4096
BATCH = 4
SEQ = 2048
DEPTH = 2
DEC_BATCH = 32
DEC_SEQ = 8
PAST_LEN = 16384
PAGE_SIZE = 128

N_META = 16
EPS = 1e-5
N_BRANCH = 4
D_MIX = D_MODEL // 4
HEAD_DIM = 64
N_HEADS = D_MIX // HEAD_DIM
N_KV = N_HEADS // 8
WINDOW = 128
ROPE_DIM = HEAD_DIM // 4
ROPE_THETA = 500000.0
NEG_INF = -1e30
GLA_HEADS = 4
GLA_DK = D_MIX // 2 // GLA_HEADS
GLA_DV = D_MIX // GLA_HEADS
GLA_RANK = 16
GLA_NORMALIZER = 16.0
GLA_CHUNK = 64
RW_HEAD = 64
RW_HEADS = D_MIX // RW_HEAD
RW_DECAY_RANK = 64
RW_A_RANK = 64
RW_G_RANK = 160
RW_IN = 3 * D_MIX + RW_DECAY_RANK + RW_A_RANK + RW_G_RANK
RW_LN_EPS = 64e-5
S5_GROUP = 16
S5_GROUPS = D_MIX // S5_GROUP
S5_STATE = 64
S5_DT_MIN = 0.001
S5_DT_MAX = 0.1
ATT_IN = D_MIX + 2 * N_KV * HEAD_DIM
GLA_IN = 2 * GLA_HEADS * GLA_DK + 2 * D_MIX + GLA_RANK
S5_IN = D_MIX
GATE_IN = N_BRANCH * D_MODEL
N_IN = ATT_IN + GLA_IN + RW_IN + S5_IN + GATE_IN
N_GROUPS = 4
EXP_PER_GROUP = 8
N_EXPERTS = N_GROUPS * EXP_PER_GROUP
TOP_K = 2
D_EXPERT = 512
MOE_BLOCK = 128

kernel_name = 'hybrid_parallel_gated_decoder_step'


def rmsnorm(x, g):
    xf = x.astype(jnp.float32)
    y = xf * lax.rsqrt(jnp.mean(xf * xf, axis=-1, keepdims=True) + EPS)
    return (y * g.astype(jnp.float32)).astype(x.dtype)


def split_points():
    widths = (D_MIX, N_KV * HEAD_DIM, N_KV * HEAD_DIM, GLA_HEADS * GLA_DK, GLA_HEADS * GLA_DK,
              D_MIX, D_MIX, GLA_RANK, RW_IN, S5_IN)
    return [int(c) for c in np.cumsum(widths)]


def rope_partial(x, pos):
    half = ROPE_DIM // 2
    inv_freq = ROPE_THETA ** (-jnp.arange(half, dtype=jnp.float32) / half)
    ang = pos.astype(jnp.float32)[:, None] * inv_freq[None, :]
    cos = jnp.cos(ang)[:, None, :]
    sin = jnp.sin(ang)[:, None, :]
    xr = x[..., :ROPE_DIM].astype(jnp.float32)
    x1, x2 = xr[..., :half], xr[..., half:]
    rot = jnp.concatenate([x1 * cos - x2 * sin, x2 * cos + x1 * sin], axis=-1).astype(x.dtype)
    return jnp.concatenate([rot, x[..., ROPE_DIM:]], axis=-1)


def sink_attention(q, k, v, valid, sinks):
    lead = q.shape[:-3]
    nq = q.shape[-3]
    qg = q.reshape(*lead, nq, N_KV, N_HEADS // N_KV, HEAD_DIM)
    logits = jnp.einsum('...qkgd,...skd->...kgqs', qg, k, preferred_element_type=jnp.float32) * HEAD_DIM ** -0.5
    logits = jnp.where(valid[..., None, None, :, :], logits, NEG_INF)
    sink = sinks.astype(jnp.float32).reshape(N_KV, N_HEADS // N_KV)[:, :, None, None]
    m = jnp.maximum(jnp.max(logits, axis=-1, keepdims=True), sink)
    pr = jnp.exp(logits - m)
    denom = jnp.sum(pr, axis=-1, keepdims=True) + jnp.exp(sink - m)
    out = jnp.einsum('...kgqs,...skd->...qkgd', (pr / denom).astype(v.dtype), v)
    return out.reshape(*lead, nq, N_HEADS * HEAD_DIM)


def attn_prompt(q, k, v, sinks):
    B, L = q.shape[0], q.shape[1]
    nb = (L - N_META) // WINDOW
    causal_meta = jnp.tril(jnp.ones((N_META, N_META), dtype=bool))
    out_meta = sink_attention(q[:, :N_META], k[:, :N_META], v[:, :N_META], causal_meta, sinks)
    qb = q[:, N_META:].reshape(B, nb, WINDOW, N_HEADS, HEAD_DIM)

    def blocks_with_prev(t):
        tb = t[:, N_META:].reshape(B, nb, WINDOW, N_KV, HEAD_DIM)
        prev = jnp.concatenate([jnp.zeros_like(tb[:, :1]), tb[:, :-1]], axis=1)
        meta = jnp.broadcast_to(t[:, None, :N_META], (B, nb, N_META, N_KV, HEAD_DIM))
        return jnp.concatenate([meta, prev, tb], axis=2)

    kb = blocks_with_prev(k)
    vb = blocks_with_prev(v)
    qi = jnp.arange(WINDOW)[:, None]
    kj = jnp.arange(2 * WINDOW)[None, :] - WINDOW
    band = (kj <= qi) & (qi - kj < WINDOW)
    not_first = (jnp.arange(nb) > 0)[:, None, None]
    band = band[None] & (not_first | (kj >= 0)[None])
    valid = jnp.concatenate([jnp.ones((nb, WINDOW, N_META), dtype=bool), band], axis=-1)
    out_real = sink_attention(qb, kb, vb, valid, sinks).reshape(B, nb * WINDOW, N_HEADS * HEAD_DIM)
    return jnp.concatenate([out_meta, out_real], axis=1)


def attn_sample(q, k, v, sinks, meta_k, meta_v, win_k, win_v):
    T = q.shape[1]
    wb = win_k.shape[1]
    q_pos = PAST_LEN + jnp.arange(T)
    win_pos = PAST_LEN - wb + jnp.arange(wb)
    dq_win = q_pos[:, None] - win_pos[None, :]
    valid_win = (win_pos[None, :] >= N_META) & (dq_win < WINDOW)
    dq_new = q_pos[:, None] - q_pos[None, :]
    valid_new = (dq_new >= 0) & (dq_new < WINDOW)
    valid = jnp.concatenate([jnp.ones((T, N_META), dtype=bool), valid_win, valid_new], axis=-1)
    keys = jnp.concatenate([meta_k.astype(k.dtype), win_k.astype(k.dtype), k], axis=1)
    vals = jnp.concatenate([meta_v.astype(v.dtype), win_v.astype(v.dtype), v], axis=1)
    out = sink_attention(q, keys, vals, valid, sinks)
    new_k = jnp.concatenate([win_k.astype(k.dtype), k], axis=1)[:, -wb:]
    new_v = jnp.concatenate([win_v.astype(v.dtype), v], axis=1)[:, -wb:]
    return out, new_k, new_v


def gla_chunked(q, k, v, log_a, h0):
    B, L = q.shape[0], q.shape[1]
    pad = (-L) % GLA_CHUNK

    def chunks(t):
        t = jnp.pad(t.astype(jnp.float32), ((0, 0), (pad, 0), (0, 0), (0, 0)))
        return t.reshape(B, (L + pad) // GLA_CHUNK, GLA_CHUNK, t.shape[2], t.shape[3])

    qc, kc, vc, gc = chunks(q), chunks(k), chunks(v), chunks(log_a)
    b = jnp.cumsum(gc, axis=2)
    b_last = b[:, :, -1:]
    q_in = qc * jnp.exp(b) * GLA_DK ** -0.5
    k_in = kc * jnp.exp(-b)
    k_out = kc * jnp.exp(b_last - b)
    causal = jnp.tril(jnp.ones((GLA_CHUNK, GLA_CHUNK), dtype=bool))
    att = jnp.where(causal, jnp.einsum('bcihk,bcjhk->bchij', q_in, k_in), 0.0)
    o_intra = jnp.einsum('bchij,bcjhv->bcihv', att, vc)
    kv = jnp.einsum('bcjhk,bcjhv->bchkv', k_out, vc)
    decay = jnp.exp(b_last[:, :, 0])

    def step(h, inp):
        d, kv_c = inp
        return h * d[..., None] + kv_c, h

    h_last, h_before = lax.scan(step, h0.astype(jnp.float32),
                                (jnp.moveaxis(decay, 1, 0), jnp.moveaxis(kv, 1, 0)))
    o_inter = jnp.einsum('bcihk,cbhkv->bcihv', q_in, h_before)
    o = (o_intra + o_inter).reshape(B, L + pad, GLA_HEADS, GLA_DV)[:, pad:]
    return o, h_last


def rwkv7(z, shift0, S0, p):
    B, L = z.shape[0], z.shape[1]
    zf = z.astype(jnp.float32)
    z_prev = jnp.concatenate([shift0[:, None].astype(jnp.float32), zf[:, :-1]], axis=1)
    zm = zf + (z_prev - zf) * p['rw_mu']
    c1 = 3 * D_MIX + RW_DECAY_RANK
    r, k, v, zw, za, zg = jnp.split(zm, [D_MIX, 2 * D_MIX, 3 * D_MIX, c1, c1 + RW_A_RANK], axis=-1)
    w = -jax.nn.softplus(-(p['rw_w0'] + jnp.tanh(zw) @ p['rw_w2'])) - 0.5
    decay = jnp.exp(-jnp.exp(w))
    a = jax.nn.sigmoid(p['rw_a0'] + za @ p['rw_a2'])
    g = jax.nn.sigmoid(zg) @ p['rw_g2']

    def heads(t):
        return t.reshape(B, L, RW_HEADS, RW_HEAD)

    kk = heads(k * p['rw_kk'])
    kk = kk / jnp.maximum(jnp.linalg.norm(kk, axis=-1, keepdims=True), 1e-12)
    k = k * (1.0 + (a - 1.0) * p['rw_ka'])
    r_h, k_h, v_h, w_h, a_h = heads(r), heads(k), heads(v), heads(decay), heads(a)

    def step(S, inp):
        r_t, w_t, k_t, v_t, kk_t, a_t = inp
        sa = jnp.einsum('bhvk,bhk->bhv', S, -kk_t)
        S = S * w_t[:, :, None, :] + sa[..., None] * (kk_t * a_t)[:, :, None, :] + v_t[..., None] * k_t[:, :, None, :]
        return S, jnp.einsum('bhvk,bhk->bhv', S, r_t)

    def tm(t):
        return jnp.moveaxis(t, 1, 0)

    S_last, y = lax.scan(step, S0.astype(jnp.float32), (tm(r_h), tm(w_h), tm(k_h), tm(v_h), tm(kk), tm(a_h)))
    y = tm(y)
    mu = jnp.mean(y, axis=-1, keepdims=True)
    var = jnp.mean(jnp.square(y - mu), axis=-1, keepdims=True)
    y = ((y - mu) * lax.rsqrt(var + RW_LN_EPS)).reshape(B, L, D_MIX) * p['rw_ln_g'] + p['rw_ln_b']
    bonus = jnp.sum(r_h * k_h * p['rw_rk'], axis=-1, keepdims=True) * v_h
    y = (y + bonus.reshape(B, L, D_MIX)) * g
    return y, S_last, zf[:, -1]


def complex_affine_combine(e1, e2):
    a1r, a1i, b1r, b1i = e1
    a2r, a2i, b2r, b2i = e2
    return (a2r * a1r - a2i * a1i, a2r * a1i + a2i * a1r,
            a2r * b1r - a2i * b1i + b2r, a2r * b1i + a2i * b1r + b2i)


def s5_layer(u, h0_re, h0_im, p):
    B, L = u.shape[0], u.shape[1]
    uf = u.astype(jnp.float32).reshape(B, L, S5_GROUPS, S5_GROUP)
    dt = jnp.exp(p['s5_log_step'].astype(jnp.float32))[:, None]
    lr = jnp.minimum(p['s5_lam_re'].astype(jnp.float32), -1e-4)
    li = p['s5_lam_im'].astype(jnp.float32)
    mag = jnp.exp(lr * dt)
    ab_re, ab_im = mag * jnp.cos(li * dt), mag * jnp.sin(li * dt)
    den = lr * lr + li * li
    f_re = ((ab_re - 1.0) * lr + ab_im * li) / den
    f_im = (ab_im * lr - (ab_re - 1.0) * li) / den
    b_re = p['s5_b_re'].astype(jnp.float32)
    b_im = p['s5_b_im'].astype(jnp.float32)
    bb_re = f_re[..., None] * b_re - f_im[..., None] * b_im
    bb_im = f_re[..., None] * b_im + f_im[..., None] * b_re
    bu_re = jnp.einsum('blgh,gph->blgp', uf, bb_re)
    bu_im = jnp.einsum('blgh,gph->blgp', uf, bb_im)
    bu_re = bu_re.at[:, 0].add(ab_re * h0_re - ab_im * h0_im)
    bu_im = bu_im.at[:, 0].add(ab_re * h0_im + ab_im * h0_re)
    a_re = jnp.broadcast_to(ab_re, bu_re.shape)
    a_im = jnp.broadcast_to(ab_im, bu_im.shape)
    _, _, h_re, h_im = lax.associative_scan(complex_affine_combine, (a_re, a_im, bu_re, bu_im), axis=1)
    y = (jnp.einsum('blgp,ghp->blgh', h_re, p['s5_c_re'])
         - jnp.einsum('blgp,ghp->blgh', h_im, p['s5_c_im'])
         + p['s5_d'] * uf)
    ya = jax.nn.gelu(y.reshape(B, L, D_MIX))
    out = ya * jax.nn.sigmoid(ya @ p['s5_glu_w'] + p['s5_glu_b'])
    return out, h_re[:, -1], h_im[:, -1]


def token_mixer(z, pos, attn_cache, gla_h0, rw_S0, rw_shift0, s5_h0_re, s5_h0_im, p):
    B, L = z.shape[0], z.shape[1]
    zq, zk, zv, gq, gk, gv, gr, glr, zrw, zs5, zgate = jnp.split(z, split_points(), axis=-1)
    q = rope_partial(zq.reshape(B, L, N_HEADS, HEAD_DIM), pos)
    k = rope_partial(zk.reshape(B, L, N_KV, HEAD_DIM), pos)
    v = zv.reshape(B, L, N_KV, HEAD_DIM)
    if attn_cache is None:
        o_att = attn_prompt(q, k, v, p['sinks'])
        att_state = (k[:, :N_META], v[:, :N_META], k[:, -WINDOW:], v[:, -WINDOW:])
    else:
        o_att, win_k, win_v = attn_sample(q, k, v, p['sinks'], *attn_cache)
        att_state = (win_k, win_v)
    log_a = jax.nn.log_sigmoid(glr.astype(jnp.float32) @ p['gla_w2'] + p['gla_b']) / GLA_NORMALIZER
    o_gla, gla_h = gla_chunked(gq.reshape(B, L, GLA_HEADS, GLA_DK), gk.reshape(B, L, GLA_HEADS, GLA_DK),
                               gv.reshape(B, L, GLA_HEADS, GLA_DV), log_a.reshape(B, L, GLA_HEADS, GLA_DK), gla_h0)
    o_gla = o_gla * lax.rsqrt(jnp.mean(o_gla * o_gla, axis=-1, keepdims=True) + EPS) * p['gla_norm']
    o_gla = o_gla.reshape(B, L, D_MIX) * jax.nn.silu(gr.astype(jnp.float32))
    o_rw, rw_S, rw_shift = rwkv7(zrw, rw_shift0, rw_S0, p)
    o_s5, s5_re, s5_im = s5_layer(zs5, s5_h0_re, s5_h0_im, p)
    branch_outs = (o_att, o_gla, o_rw, o_s5)
    merged = jnp.zeros((B, L, D_MODEL), z.dtype)
    for i in range(N_BRANCH):
        gate = jax.nn.sigmoid(zgate[..., i * D_MODEL:(i + 1) * D_MODEL])
        merged = merged + gate * (branch_outs[i].astype(z.dtype) @ p['w_branch'][i])
    return merged @ p['w_out'], att_state, gla_h, rw_S, rw_shift, s5_re, s5_im


def hier_moe(x, w_rg, b_rg, w_re, b_re, w1, w3, w2):
    T = x.shape[0]
    xf = x.astype(jnp.float32)
    lg = xf @ w_rg.astype(jnp.float32) + b_rg
    pg = jax.nn.softmax(lg, axis=-1)
    g_top = jnp.argmax(lg, axis=-1)
    le = (xf @ w_re.astype(jnp.float32) + b_re).reshape(T, N_GROUPS, EXP_PER_GROUP)
    rows = jnp.arange(T)
    le_sel = le[rows, g_top]
    top_l, top_i = lax.top_k(le_sel, TOP_K)
    w = jax.nn.softmax(top_l, axis=-1) * pg[rows, g_top][:, None]
    e_idx = g_top[:, None] * EXP_PER_GROUP + top_i
    A = T * TOP_K
    e_flat = e_idx.reshape(A).astype(jnp.int32)
    tok_flat = jnp.repeat(jnp.arange(T, dtype=jnp.int32), TOP_K)
    w_flat = w.reshape(A)
    order = jnp.argsort(e_flat)
    e_s, tok_s, w_s = e_flat[order], tok_flat[order], w_flat[order]
    counts = jnp.zeros((N_EXPERTS,), jnp.int32).at[e_flat].add(1)
    starts = jnp.cumsum(counts) - counts
    padded = (counts + MOE_BLOCK - 1) // MOE_BLOCK * MOE_BLOCK
    pad_end = jnp.cumsum(padded)
    pad_start = pad_end - padded
    dest = pad_start[e_s] + (jnp.arange(A, dtype=jnp.int32) - starts[e_s])
    n_blocks = -(-(A + N_EXPERTS * (MOE_BLOCK - 1)) // MOE_BLOCK)
    n_slots = n_blocks * MOE_BLOCK
    slot_tok = jnp.full((n_slots,), T, jnp.int32).at[dest].set(tok_s)
    slot_w = jnp.zeros((n_slots,), jnp.float32).at[dest].set(w_s)
    blk_exp = jnp.minimum(jnp.searchsorted(pad_end, jnp.arange(n_blocks, dtype=jnp.int32) * MOE_BLOCK, side='right'),
                          N_EXPERTS - 1)
    x_pad = jnp.concatenate([x, jnp.zeros((1, x.shape[1]), x.dtype)], axis=0)
    xb = x_pad[slot_tok].reshape(n_blocks, MOE_BLOCK, x.shape[1])

    def expert_block(args):
        xblk, e = args
        h = jax.nn.silu(xblk @ w1[e]) * (xblk @ w3[e])
        return h @ w2[e]

    yb = lax.map(expert_block, (xb, blk_exp)).reshape(n_slots, x.shape[1])
    y = jnp.zeros((T + 1, x.shape[1]), x.dtype).at[slot_tok].add(yb.astype(x.dtype) * slot_w[:, None].astype(x.dtype))
    return y[:T]


def setup_inputs(seed: int = 0) -> dict:
    key = jax.random.key(seed)
    keys = iter(jax.random.split(key, 64))
    f32 = jnp.float32

    def nrm(shape, scale=1.0):
        return scale * jax.random.normal(next(keys), shape, f32)

    def gain(shape):
        return 1.0 + nrm(shape, 0.01)

    wb = min(WINDOW, PAST_LEN)
    lam_im0 = jnp.pi * jnp.arange(S5_STATE, dtype=f32)
    return {
        'x_prompt': nrm((BATCH, SEQ, D_MODEL)),
        'x_sample': nrm((DEC_BATCH, DEC_SEQ, D_MODEL)),
        'cache_attn_meta_k': nrm((DEPTH, DEC_BATCH, N_META, N_KV, HEAD_DIM)),
        'cache_attn_meta_v': nrm((DEPTH, DEC_BATCH, N_META, N_KV, HEAD_DIM)),
        'cache_attn_win_k': nrm((DEPTH, DEC_BATCH, wb, N_KV, HEAD_DIM)),
        'cache_attn_win_v': nrm((DEPTH, DEC_BATCH, wb, N_KV, HEAD_DIM)),
        'state_gla': nrm((DEPTH, DEC_BATCH, GLA_HEADS, GLA_DK, GLA_DV), 0.5),
        'state_rwkv': nrm((DEPTH, DEC_BATCH, RW_HEADS, RW_HEAD, RW_HEAD), 0.5),
        'state_rwkv_shift': nrm((DEPTH, DEC_BATCH, RW_IN)),
        'state_s5_re': nrm((DEPTH, DEC_BATCH, S5_GROUPS, S5_STATE), 0.1),
        'state_s5_im': nrm((DEPTH, DEC_BATCH, S5_GROUPS, S5_STATE), 0.1),
        'meta_tokens': nrm((N_META, D_MODEL)),
        'norm_mix': gain((DEPTH, D_MODEL)),
        'w_in': nrm((DEPTH, D_MODEL, N_IN), D_MODEL ** -0.5),
        'w_gla_gate': nrm((DEPTH, GLA_RANK, GLA_HEADS * GLA_DK), GLA_RANK ** -0.5),
        'b_gla_gate': nrm((DEPTH, GLA_HEADS * GLA_DK), 0.1),
        'gla_norm': gain((DEPTH, GLA_DV)),
        'attn_sinks': nrm((DEPTH, N_HEADS)),
        'rw_mu': jax.random.uniform(next(keys), (DEPTH, RW_IN), f32),
        'rw_w0': nrm((DEPTH, D_MIX), 0.5) - 0.5,
        'rw_w2': nrm((DEPTH, RW_DECAY_RANK, D_MIX), 0.1 * RW_DECAY_RANK ** -0.5),
        'rw_a0': nrm((DEPTH, D_MIX), 0.1),
        'rw_a2': nrm((DEPTH, RW_A_RANK, D_MIX), 0.1 * RW_A_RANK ** -0.5),
        'rw_g2': nrm((DEPTH, RW_G_RANK, D_MIX), RW_G_RANK ** -0.5),
        'rw_kk': 0.85 + nrm((DEPTH, D_MIX), 0.05),
        'rw_ka': 1.0 + nrm((DEPTH, D_MIX), 0.05),
        'rw_rk': nrm((DEPTH, RW_HEADS, RW_HEAD), 0.1),
        'rw_ln_g': gain((DEPTH, D_MIX)),
        'rw_ln_b': nrm((DEPTH, D_MIX), 0.01),
        's5_lam_re': -0.5 + nrm((DEPTH, S5_GROUPS, S5_STATE), 0.01),
        's5_lam_im': lam_im0 + nrm((DEPTH, S5_GROUPS, S5_STATE), 0.01),
        's5_log_step': jax.random.uniform(next(keys), (DEPTH, S5_GROUPS), f32, math.log(S5_DT_MIN), math.log(S5_DT_MAX)),
        's5_b_re': nrm((DEPTH, S5_GROUPS, S5_STATE, S5_GROUP), (2 * S5_GROUP) ** -0.5),
        's5_b_im': nrm((DEPTH, S5_GROUPS, S5_STATE, S5_GROUP), (2 * S5_GROUP) ** -0.5),
        's5_c_re': nrm((DEPTH, S5_GROUPS, S5_GROUP, S5_STATE), (2 * S5_STATE) ** -0.5),
        's5_c_im': nrm((DEPTH, S5_GROUPS, S5_GROUP, S5_STATE), (2 * S5_STATE) ** -0.5),
        's5_d': nrm((DEPTH, S5_GROUPS, S5_GROUP)),
        's5_glu_w': nrm((DEPTH, D_MIX, D_MIX), D_MIX ** -0.5),
        's5_glu_b': nrm((DEPTH, D_MIX), 0.01),
        'w_branch': nrm((DEPTH, N_BRANCH, D_MIX, D_MODEL), D_MIX ** -0.5),
        'w_out': nrm((DEPTH, D_MODEL, D_MODEL), D_MODEL ** -0.5),
        'norm_moe': gain((DEPTH, D_MODEL)),
        'w_router_group': nrm((DEPTH, D_MODEL, N_GROUPS), D_MODEL ** -0.5),
        'b_router_group': nrm((DEPTH, N_GROUPS), 0.01),
        'w_router_expert': nrm((DEPTH, D_MODEL, N_EXPERTS), D_MODEL ** -0.5),
        'b_router_expert': nrm((DEPTH, N_EXPERTS), 0.01),
        'w_exp_gate': nrm((DEPTH, N_EXPERTS, D_MODEL, D_EXPERT), D_MODEL ** -0.5),
        'w_exp_up': nrm((DEPTH, N_EXPERTS, D_MODEL, D_EXPERT), D_MODEL ** -0.5),
        'w_exp_down': nrm((DEPTH, N_EXPERTS, D_EXPERT, D_MODEL), D_EXPERT ** -0.5),
        'norm_final': gain((D_MODEL,)),
    }


def reference(x_prompt, x_sample, cache_attn_meta_k, cache_attn_meta_v, cache_attn_win_k, cache_attn_win_v,
              state_gla, state_rwkv, state_rwkv_shift, state_s5_re, state_s5_im,
              meta_tokens, norm_mix, w_in, w_gla_gate, b_gla_gate, gla_norm, attn_sinks,
              rw_mu, rw_w0, rw_w2, rw_a0, rw_a2, rw_g2, rw_kk, rw_ka, rw_rk, rw_ln_g, rw_ln_b,
              s5_lam_re, s5_lam_im, s5_log_step, s5_b_re, s5_b_im, s5_c_re, s5_c_im, s5_d, s5_glu_w, s5_glu_b,
              w_branch, w_out, norm_moe, w_router_group, b_router_group, w_router_expert, b_router_expert,
              w_exp_gate, w_exp_up, w_exp_down, norm_final):
    f32 = jnp.float32
    B = x_prompt.shape[0]
    DB, T = x_sample.shape[0], x_sample.shape[1]
    meta = jnp.broadcast_to(meta_tokens.astype(x_prompt.dtype)[None], (B, N_META, D_MODEL))
    xp = jnp.concatenate([meta, x_prompt], axis=1)
    xs = x_sample
    L = xp.shape[1]
    pos_p = jnp.arange(L, dtype=jnp.int32)
    pos_s = PAST_LEN + jnp.arange(T, dtype=jnp.int32)
    gla0 = jnp.zeros((B, GLA_HEADS, GLA_DK, GLA_DV), f32)
    rw0 = jnp.zeros((B, RW_HEADS, RW_HEAD, RW_HEAD), f32)
    sh0 = jnp.zeros((B, RW_IN), f32)
    s50 = jnp.zeros((B, S5_GROUPS, S5_STATE), f32)
    rec = [[] for _ in range(16)]
    for l in range(DEPTH):
        p = {'sinks': attn_sinks[l], 'gla_w2': w_gla_gate[l], 'gla_b': b_gla_gate[l], 'gla_norm': gla_norm[l],
             'rw_mu': rw_mu[l], 'rw_w0': rw_w0[l], 'rw_w2': rw_w2[l], 'rw_a0': rw_a0[l], 'rw_a2': rw_a2[l],
             'rw_g2': rw_g2[l], 'rw_kk': rw_kk[l], 'rw_ka': rw_ka[l], 'rw_rk': rw_rk[l],
             'rw_ln_g': rw_ln_g[l], 'rw_ln_b': rw_ln_b[l],
             's5_lam_re': s5_lam_re[l], 's5_lam_im': s5_lam_im[l], 's5_log_step': s5_log_step[l],
             's5_b_re': s5_b_re[l], 's5_b_im': s5_b_im[l], 's5_c_re': s5_c_re[l], 's5_c_im': s5_c_im[l],
             's5_d': s5_d[l], 's5_glu_w': s5_glu_w[l], 's5_glu_b': s5_glu_b[l],
             'w_branch': w_branch[l], 'w_out': w_out[l]}
        mp, att_p, gla_p, rw_p, sh_p, re_p, im_p = token_mixer(
            rmsnorm(xp, norm_mix[l]) @ w_in[l], pos_p, None, gla0, rw0, sh0, s50, s50, p)
        ms, att_s, gla_s, rw_s, sh_s, re_s, im_s = token_mixer(
            rmsnorm(xs, norm_mix[l]) @ w_in[l], pos_s,
            (cache_attn_meta_k[l], cache_attn_meta_v[l], cache_attn_win_k[l], cache_attn_win_v[l]),
            state_gla[l], state_rwkv[l], state_rwkv_shift[l], state_s5_re[l], state_s5_im[l], p)
        xp = xp + mp.astype(xp.dtype)
        xs = xs + ms.astype(xs.dtype)
        tokens = jnp.concatenate([rmsnorm(xp, norm_moe[l]).reshape(B * L, D_MODEL),
                                  rmsnorm(xs, norm_moe[l]).reshape(DB * T, D_MODEL)], axis=0)
        y = hier_moe(tokens, w_router_group[l], b_router_group[l], w_router_expert[l], b_router_expert[l],
                     w_exp_gate[l], w_exp_up[l], w_exp_down[l])
        xp = xp + y[:B * L].reshape(B, L, D_MODEL).astype(xp.dtype)
        xs = xs + y[B * L:].reshape(DB, T, D_MODEL).astype(xs.dtype)
        vals = (*att_p, gla_p, rw_p, sh_p, re_p, im_p, *att_s, gla_s, rw_s, sh_s, re_s, im_s)
        for i in range(16):
            rec[i].append(vals[i])
    y_prompt = rmsnorm(xp, norm_final)[:, N_META:]
    y_sample = rmsnorm(xs, norm_final)
    (p_meta_k, p_meta_v, p_win_k, p_win_v, p_gla, p_rwkv, p_shift, p_s5_re, p_s5_im,
     s_win_k, s_win_v, s_gla, s_rwkv, s_shift, s_s5_re, s_s5_im) = [jnp.stack(r) for r in rec]
    return (y_prompt, y_sample, p_meta_k, p_meta_v, p_win_k, p_win_v, p_gla, p_rwkv, p_shift, p_s5_re, p_s5_im,
            s_win_k, s_win_v, s_gla, s_rwkv, s_shift, s_s5_re, s_s5_im)
```

```python
import functools
import math

import numpy as np
import jax
import jax.numpy as jnp
from jax import lax
from jax.experimental import pallas as pl
from jax.experimental.pallas import tpu as pltpu

F32 = jnp.float32
BF16 = jnp.bfloat16

D_MODEL = 4096
N_META = 16
EPS = 1e-5
D_MIX = 1024
HEAD_DIM = 64
N_HEADS = 16
N_KV = 2
WINDOW = 128
ROPE_DIM = 16
ROPE_THETA = 500000.0
NEG_INF = -1e30
GLA_HEADS = 4
GLA_DK = 128
GLA_DV = 256
GLA_RANK = 16
GLA_NORMALIZER = 16.0
GLA_CHUNK = 64
RW_HEAD = 64
RW_HEADS = 16
RW_DECAY_RANK = 64
RW_A_RANK = 64
RW_G_RANK = 160
RW_IN = 3 * D_MIX + RW_DECAY_RANK + RW_A_RANK + RW_G_RANK
RW_LN_EPS = 64e-5
S5_GROUP = 16
S5_GROUPS = 64
S5_STATE = 64
N_GROUPS = 4
EXP_PER_GROUP = 8
N_EXPERTS = 32
TOP_K = 2
D_EXPERT = 512
MOE_BLOCK = 128
PAST_LEN = 16384
N_BRANCH = 4

LANES = 128
SUBLANES = 8
VMEM_BUDGET = 56 * 1024 * 1024

Z_Q = 0
Z_RR = 1024
Z_RK = 2048
Z_RV = 3072
Z_S5 = 4096
Z_GV = 5120
Z_GR = 6144
Z_LORA = 7168
Z_GQ = 7680
Z_GK = 8192
Z_AK = 8704
Z_AV = 8832
Z_GLR = 8960
NZ_MIX = 9216
LORA_W = 512

_SRC_Q, _SRC_K, _SRC_V, _SRC_GQ, _SRC_GK, _SRC_GV, _SRC_GR, _SRC_GLR = 0, 1024, 1152, 1280, 1792, 2304, 3328, 4352
_SRC_RW = 4368
_SRC_S5 = _SRC_RW + RW_IN
_SRC_GATE = _SRC_S5 + D_MIX


def _cparams(sem, vmem=VMEM_BUDGET):
    return pltpu.CompilerParams(dimension_semantics=sem, vmem_limit_bytes=int(vmem))


def _bdot(a, b):
    return jnp.dot(a.astype(BF16), b.astype(BF16), preferred_element_type=F32)


def _bdot_nt(a, b):
    return lax.dot_general(a.astype(BF16), b.astype(BF16), (((1,), (1,)), ((), ())), preferred_element_type=F32)


def _bdot_tn(a, b):
    return lax.dot_general(a.astype(BF16), b.astype(BF16), (((0,), (0,)), ((), ())), preferred_element_type=F32)


def _split2(x):
    hi = x.astype(BF16)
    lo = (x - hi.astype(F32)).astype(BF16)
    return hi, lo


def _dot2(x, w_bf16):
    hi, lo = _split2(x)
    return (jnp.dot(hi, w_bf16, preferred_element_type=F32) + jnp.dot(lo, w_bf16, preferred_element_type=F32))


def _sigmoid(x):
    return 1.0 / (1.0 + jnp.exp(-x))


def _silu(x):
    return x * _sigmoid(x)


def _softplus(x):
    return jnp.maximum(x, 0.0) + jnp.log(1.0 + jnp.exp(-jnp.abs(x)))


def _rmsnorm_body(x_ref, g_ref, o_ref):
    x = x_ref[...]
    y = x * lax.rsqrt(jnp.mean(x * x, axis=-1, keepdims=True) + EPS)
    o_ref[...] = (y * g_ref[...]).astype(o_ref.dtype)


def rmsnorm_rows(x, g3, layer, out_dtype, tm):
    m, d = x.shape
    return pl.pallas_call(
        _rmsnorm_body,
        grid=(m // tm,),
        in_specs=[pl.BlockSpec((tm, d), lambda i: (i, 0)),
                  pl.BlockSpec((None, 1, d), lambda i: (layer, 0, 0))],
        out_specs=pl.BlockSpec((tm, d), lambda i: (i, 0)),
        out_shape=jax.ShapeDtypeStruct((m, d), out_dtype),
        compiler_params=_cparams(("parallel",)),
        name="rmsnorm_rows",
    )(x, g3)


def _mm_body(a_ref, w_ref, o_ref):
    o_ref[...] = jnp.dot(a_ref[...], w_ref[...].astype(BF16), preferred_element_type=F32).astype(o_ref.dtype)


def _mm_res_body(a_ref, w_ref, r_ref, o_ref):
    o_ref[...] = r_ref[...] + jnp.dot(a_ref[...], w_ref[...].astype(BF16), preferred_element_type=F32)


def matmul_rows(a, w3, layer, *, tm, tn, residual=None, name="matmul_rows"):
    m, k = a.shape
    n = w3.shape[2]
    in_specs = [pl.BlockSpec((tm, k), lambda i, j: (i, 0)),
                pl.BlockSpec((None, k, tn), lambda i, j: (layer, 0, j))]
    args = [a, w3]
    body = _mm_body
    if residual is not None:
        in_specs.append(pl.BlockSpec((tm, tn), lambda i, j: (i, j)))
        args.append(residual)
        body = _mm_res_body
    return pl.pallas_call(
        body,
        grid=(m // tm, n // tn),
        in_specs=in_specs,
        out_specs=pl.BlockSpec((tm, tn), lambda i, j: (i, j)),
        out_shape=jax.ShapeDtypeStruct((m, n), F32),
        compiler_params=_cparams(("parallel", "arbitrary")),
        name=name,
    )(*args)


def _rope_tables(pos):
    half = ROPE_DIM // 2
    inv_freq = ROPE_THETA ** (-jnp.arange(half, dtype=F32) / half)
    ang = pos.astype(F32)[:, None] * inv_freq[None, :]
    cos, sin = jnp.cos(ang), jnp.sin(ang)
    n = pos.shape[0]
    ones = jnp.ones((n, HEAD_DIM - ROPE_DIM), F32)
    zeros = jnp.zeros((n, HEAD_DIM - ROPE_DIM), F32)
    c64 = jnp.concatenate([cos, cos, ones], axis=1)
    s64 = jnp.concatenate([-sin, sin, zeros], axis=1)
    return jnp.concatenate([c64, c64], axis=1), jnp.concatenate([s64, s64], axis=1)


def _rope(x, c, s):
    lane = lax.broadcasted_iota(jnp.int32, x.shape, 1) % HEAD_DIM
    partner = jnp.where(lane < ROPE_DIM // 2,
                        pltpu.roll(x, LANES - ROPE_DIM // 2, 1),
                        pltpu.roll(x, ROPE_DIM // 2, 1))
    return x * c + partner * s


def _dup_half(x, g):
    lane = lax.broadcasted_iota(jnp.int32, x.shape, 1)
    sw = pltpu.roll(x, HEAD_DIM, 1)
    if g == 0:
        return jnp.where(lane < HEAD_DIM, x, sw)
    return jnp.where(lane < HEAD_DIM, sw, x)


def _attn_frame(q_rows, cq, sq, kcat, vcat, valid, sink_ref, write):
    r = q_rows.shape[0]
    lane = lax.broadcasted_iota(jnp.int32, (r, LANES), 1)
    lane_v = lax.broadcasted_iota(jnp.int32, (3 * WINDOW, LANES), 1)
    for p in range(N_HEADS // 2):
        g = (2 * p) // (N_HEADS // N_KV)
        qp = _rope(q_rows[:, LANES * p:LANES * (p + 1)], cq, sq)
        acc = jnp.zeros((r, LANES), F32)
        for hh in range(2):
            h = 2 * p + hh
            in_half = (lane >= HEAD_DIM) if hh else (lane < HEAD_DIM)
            qm = jnp.where(in_half, qp, 0.0)
            logits = _bdot_nt(qm, kcat[g]) * (HEAD_DIM ** -0.5)
            logits = jnp.where(valid, logits, NEG_INF)
            sink = sink_ref[h]
            m = jnp.maximum(jnp.max(logits, axis=-1, keepdims=True), sink)
            pr = jnp.exp(logits - m)
            denom = jnp.sum(pr, axis=-1, keepdims=True) + jnp.exp(sink - m)
            v_half = jnp.where((lane_v >= HEAD_DIM) if hh else (lane_v < HEAD_DIM), vcat[g], 0.0)
            acc = acc + _bdot(pr / denom, v_half)
        write(p, acc)


def _frame_masks(r, no_prev):
    qi = lax.broadcasted_iota(jnp.int32, (r, 3 * WINDOW), 0)
    c = lax.broadcasted_iota(jnp.int32, (r, 3 * WINDOW), 1)
    prev_ok = (c < WINDOW) & (c > qi + no_prev * WINDOW)
    cur_ok = (c >= WINDOW) & (c < 2 * WINDOW) & (c - WINDOW <= qi)
    meta_ok = (c >= 2 * WINDOW) & (c < 2 * WINDOW + N_META)
    return prev_ok | cur_ok | meta_ok


def _attn_prompt_body(sink_ref, q_ref, k_ref, v_ref, c_ref, s_ref, o_ref, kr_ref):
    seq = q_ref.shape[0]
    nb = (seq - N_META) // WINDOW
    kr_ref[...] = _rope(k_ref[...], c_ref[...], s_ref[...])
    zpad = jnp.zeros((WINDOW - N_META, LANES), F32)
    kmeta = jnp.concatenate([kr_ref[0:N_META, :], zpad], axis=0)
    vmeta = jnp.concatenate([v_ref[0:N_META, :], zpad], axis=0)
    kmeta2 = [_dup_half(kmeta, g) for g in range(N_KV)]
    vmeta2 = [_dup_half(vmeta, g) for g in range(N_KV)]

    qi = lax.broadcasted_iota(jnp.int32, (N_META, 3 * WINDOW), 0)
    c = lax.broadcasted_iota(jnp.int32, (N_META, 3 * WINDOW), 1)
    valid_meta = (c >= 2 * WINDOW) & (c - 2 * WINDOW <= qi)
    kcat = [jnp.concatenate([kmeta2[g], kmeta2[g], kmeta2[g]], axis=0) for g in range(N_KV)]
    vcat = [jnp.concatenate([vmeta2[g], vmeta2[g], vmeta2[g]], axis=0) for g in range(N_KV)]

    def write_meta(p, val):
        o_ref[0:N_META, LANES * p:LANES * (p + 1)] = val

    _attn_frame(q_ref[0:N_META, :], c_ref[0:N_META, :], s_ref[0:N_META, :], kcat, vcat, valid_meta,
                sink_ref, write_meta)

    def frame(n, carry):
        r0 = pl.multiple_of(N_META + WINDOW * n, SUBLANES)
        p0 = pl.multiple_of(jnp.maximum(r0 - WINDOW, 0), SUBLANES)
        kprev, kcur = kr_ref[pl.ds(p0, WINDOW), :], kr_ref[pl.ds(r0, WINDOW), :]
        vprev, vcur = v_ref[pl.ds(p0, WINDOW), :], v_ref[pl.ds(r0, WINDOW), :]
        kc = [jnp.concatenate([_dup_half(kprev, g), _dup_half(kcur, g), kmeta2[g]], axis=0) for g in range(N_KV)]
        vc = [jnp.concatenate([_dup_half(vprev, g), _dup_half(vcur, g), vmeta2[g]], axis=0) for g in range(N_KV)]
        valid = _frame_masks(WINDOW, jnp.where(n == 0, 1, 0))

        def write(p, val):
            o_ref[pl.ds(r0, WINDOW), LANES * p:LANES * (p + 1)] = val

        _attn_frame(q_ref[pl.ds(r0, WINDOW), :], c_ref[pl.ds(r0, WINDOW), :], s_ref[pl.ds(r0, WINDOW), :],
                    kc, vc, valid, sink_ref, write)
        return carry

    lax.fori_loop(0, nb, frame, 0)


def attn_prompt(z, sinks, cos_t, sin_t, n_batch, seq):
    grid_spec = pltpu.PrefetchScalarGridSpec(
        num_scalar_prefetch=0,
        grid=(n_batch,),
        in_specs=[pl.BlockSpec(memory_space=pltpu.SMEM),
                  pl.BlockSpec((seq, D_MIX), lambda b: (b, Z_Q // D_MIX)),
                  pl.BlockSpec((seq, LANES), lambda b: (b, Z_AK // LANES)),
                  pl.BlockSpec((seq, LANES), lambda b: (b, Z_AV // LANES)),
                  pl.BlockSpec((seq, LANES), lambda b: (0, 0)),
                  pl.BlockSpec((seq, LANES), lambda b: (0, 0))],
        out_specs=[pl.BlockSpec((seq, D_MIX), lambda b: (b, 0)),
                   pl.BlockSpec((seq, LANES), lambda b: (b, 0))],
    )
    return pl.pallas_call(
        _attn_prompt_body,
        grid_spec=grid_spec,
        out_shape=[jax.ShapeDtypeStruct((n_batch * seq, D_MIX), F32),
                   jax.ShapeDtypeStruct((n_batch * seq, LANES), F32)],
        compiler_params=_cparams(("parallel",)),
        name="attn_prompt",
    )(sinks, z, z, z, cos_t, sin_t)


def _attn_sample_body(sink_ref, q_ref, k_ref, v_ref, c_ref, s_ref, mk_ref, mv_ref, wk_ref, wv_ref, o_ref, kr_ref,
                      *, n_seq, t_len):
    kr_ref[...] = _rope(k_ref[...], jnp.concatenate([c_ref[...]] * n_seq, axis=0),
                        jnp.concatenate([s_ref[...]] * n_seq, axis=0))
    zpad_m = jnp.zeros((WINDOW - N_META, LANES), F32)
    zpad_c = jnp.zeros((WINDOW - t_len, LANES), F32)
    valid = _frame_masks(t_len, 0)
    for s in range(n_seq):
        rows = slice(s * t_len, (s + 1) * t_len)
        kmeta = jnp.concatenate([mk_ref[s], zpad_m], axis=0)
        vmeta = jnp.concatenate([mv_ref[s], zpad_m], axis=0)
        kcur = jnp.concatenate([kr_ref[rows, :], zpad_c], axis=0)
        vcur = jnp.concatenate([v_ref[rows, :], zpad_c], axis=0)
        kprev, vprev = wk_ref[s], wv_ref[s]
        kc = [jnp.concatenate([_dup_half(kprev, g), _dup_half(kcur, g), _dup_half(kmeta, g)], axis=0)
              for g in range(N_KV)]
        vc = [jnp.concatenate([_dup_half(vprev, g), _dup_half(vcur, g), _dup_half(vmeta, g)], axis=0)
              for g in range(N_KV)]

        def write(p, val, rows=rows):
            o_ref[rows, LANES * p:LANES * (p + 1)] = val

        _attn_frame(q_ref[rows, :], c_ref[...], s_ref[...], kc, vc, valid, sink_ref, write)


def attn_sample(z, row0, sinks, cos_t, sin_t, meta_k, meta_v, win_k, win_v, n_seq_total, t_len, n_seq=8):
    rows = n_seq * t_len
    blk0 = row0 // rows
    grid_spec = pltpu.PrefetchScalarGridSpec(
        num_scalar_prefetch=0,
        grid=(n_seq_total // n_seq,),
        in_specs=[pl.BlockSpec(memory_space=pltpu.SMEM),
                  pl.BlockSpec((rows, D_MIX), lambda i: (blk0 + i, Z_Q // D_MIX)),
                  pl.BlockSpec((rows, LANES), lambda i: (blk0 + i, Z_AK // LANES)),
                  pl.BlockSpec((rows, LANES), lambda i: (blk0 + i, Z_AV // LANES)),
                  pl.BlockSpec((t_len, LANES), lambda i: (0, 0)),
                  pl.BlockSpec((t_len, LANES), lambda i: (0, 0)),
                  pl.BlockSpec((n_seq, N_META, LANES), lambda i: (i, 0, 0)),
                  pl.BlockSpec((n_seq, N_META, LANES), lambda i: (i, 0, 0)),
                  pl.BlockSpec((n_seq, WINDOW, LANES), lambda i: (i, 0, 0)),
                  pl.BlockSpec((n_seq, WINDOW, LANES), lambda i: (i, 0, 0))],
        out_specs=[pl.BlockSpec((rows, D_MIX), lambda i: (i, 0)),
                   pl.BlockSpec((rows, LANES), lambda i: (i, 0))],
    )
    return pl.pallas_call(
        functools.partial(_attn_sample_body, n_seq=n_seq, t_len=t_len),
        grid_spec=grid_spec,
        out_shape=[jax.ShapeDtypeStruct((n_seq_total * t_len, D_MIX), F32),
                   jax.ShapeDtypeStruct((n_seq_total * t_len, LANES), F32)],
        compiler_params=_cparams(("parallel",)),
        name="attn_sample",
    )(sinks, z, z, z, cos_t, sin_t, meta_k, meta_v, win_k, win_v)


def _gla_body(q_ref, k_ref, v_ref, r_ref, lr_ref, w2_ref, b_ref, norm_ref, h0_ref, o_ref, h_ref, ht_ref):
    seq = q_ref.shape[0]
    cs = GLA_CHUNK
    ht_ref[...] = h0_ref[...].T
    row = lax.broadcasted_iota(jnp.int32, (cs, cs), 0)
    col = lax.broadcasted_iota(jnp.int32, (cs, cs), 1)
    tril = row >= col
    tril_bf = jnp.where(tril, 1.0, 0.0).astype(BF16)

    def chunk(r0, nvalid):
        nload = min(cs, seq)

        def load(ref):
            x = ref[pl.ds(r0, nload), :]
            if nload < cs:
                x = jnp.concatenate([x, jnp.zeros((cs - nload, x.shape[1]), F32)], axis=0)
            return x

        q, k, v, gr, lr = load(q_ref), load(k_ref), load(v_ref), load(r_ref), load(lr_ref)
        pre = _bdot(lr, w2_ref[...]) + b_ref[...]
        la = (jnp.minimum(pre, 0.0) - jnp.log(1.0 + jnp.exp(-jnp.abs(pre)))) / GLA_NORMALIZER
        if nvalid < cs:
            live = lax.broadcasted_iota(jnp.int32, (cs, 1), 0) < nvalid
            la = jnp.where(live, la, 0.0)
            k = jnp.where(live, k, 0.0)
            v = jnp.where(live, v, 0.0)
        la_hi, la_lo = _split2(la)
        bc = (jnp.dot(tril_bf, la_hi, preferred_element_type=F32)
              + jnp.dot(tril_bf, la_lo, preferred_element_type=F32))
        b_last = bc[cs - 1:cs, :]
        q_in = q * jnp.exp(bc) * (GLA_DK ** -0.5)
        k_in = k * jnp.exp(-bc)
        k_out = k * jnp.exp(b_last - bc)
        att = jnp.where(tril, _bdot_nt(q_in, k_in), 0.0)
        ht = ht_ref[...]
        o = _bdot(att, v) + _bdot_nt(q_in, ht)
        ht_ref[...] = ht * jnp.exp(b_last) + _bdot_tn(v, k_out)
        o = o * lax.rsqrt(jnp.mean(o * o, axis=-1, keepdims=True) + EPS) * norm_ref[...]
        o = o * _silu(gr)
        o_ref[pl.ds(r0, nvalid), :] = o[0:nvalid, :]

    if seq < cs:
        chunk(0, seq)
    else:
        lead = seq % cs
        if lead:
            chunk(0, lead)

        def step(c, carry):
            chunk(pl.multiple_of(lead + cs * c, SUBLANES), cs)
            return carry

        lax.fori_loop(0, seq // cs, step, 0)
    h_ref[...] = ht_ref[...].T


def gla(z, blk0, n_seq, seq, w2p, b3, norm3, layer, h0):
    def zspec(width, col0):
        return pl.BlockSpec((seq, width), lambda s, h: (blk0 + s, col0 // width + h))

    return pl.pallas_call(
        _gla_body,
        grid=(n_seq, GLA_HEADS),
        in_specs=[zspec(GLA_DK, Z_GQ), zspec(GLA_DK, Z_GK), zspec(GLA_DV, Z_GV), zspec(GLA_DV, Z_GR),
                  pl.BlockSpec((seq, LANES), lambda s, h: (blk0 + s, Z_GLR // LANES)),
                  pl.BlockSpec((None, LANES, GLA_DK), lambda s, h: (layer, 0, h)),
                  pl.BlockSpec((None, 1, GLA_DK), lambda s, h: (layer, 0, h)),
                  pl.BlockSpec((None, 1, GLA_DV), lambda s, h: (layer, 0, 0)),
                  pl.BlockSpec((None, None, GLA_DK, GLA_DV), lambda s, h: (s, h, 0, 0))],
        out_specs=[pl.BlockSpec((seq, GLA_DV), lambda s, h: (s, h)),
                   pl.BlockSpec((None, None, GLA_DK, GLA_DV), lambda s, h: (s, h, 0, 0))],
        out_shape=[jax.ShapeDtypeStruct((n_seq * seq, D_MIX), F32),
                   jax.ShapeDtypeStruct((n_seq, GLA_HEADS, GLA_DK, GLA_DV), F32)],
        scratch_shapes=[pltpu.VMEM((GLA_DV, GLA_DK), F32)],
        compiler_params=_cparams(("parallel", "parallel")),
        name="gla",
    )(z, z, z, z, z, w2p, b3, norm3, h0)


def _seg_ones(n, seg):
    r = lax.broadcasted_iota(jnp.int32, (n, n), 0) // seg
    c = lax.broadcasted_iota(jnp.int32, (n, n), 1) // seg
    return jnp.where(r == c, 1.0, 0.0).astype(BF16)


def _rw_pre_body(zr_ref, zk_ref, zv_ref, zl_ref, pr_ref, pk_ref, pv_ref, pl_ref,
                 mur_ref, muk_ref, muv_ref, mul_ref, w0_ref, w2_ref, a0_ref, a2_ref, g2_ref,
                 kk_ref, ka_ref, rk_ref,
                 r_out, w_out, k_out, v_out, kk_out, b_out, g_out, bonus_out):
    rows = zr_ref.shape[0]
    first = lax.broadcasted_iota(jnp.int32, (rows, 1), 0) == 0

    def shifted(z_ref, p_ref, mu_ref):
        z = z_ref[...]
        prev = jnp.where(first, p_ref[...], pltpu.roll(z, 1, 0))
        return z + (prev - z) * mu_ref[...]

    r = shifted(zr_ref, pr_ref, mur_ref)
    k = shifted(zk_ref, pk_ref, muk_ref)
    v = shifted(zv_ref, pv_ref, muv_ref)
    lo = shifted(zl_ref, pl_ref, mul_ref)
    zw, za, zg = lo[:, 0:LANES], lo[:, LANES:2 * LANES], lo[:, 2 * LANES:4 * LANES]
    w = -_softplus(-(w0_ref[...] + _bdot(jnp.tanh(zw), w2_ref[...]))) - 0.5
    decay = jnp.exp(-jnp.exp(w))
    a = _sigmoid(a0_ref[...] + _bdot(za, a2_ref[...]))
    g = _bdot(_sigmoid(zg), g2_ref[...])
    ones = _seg_ones(D_MIX, RW_HEAD)
    kkr = k * kk_ref[...]
    nrm = jnp.sqrt(_dot2(kkr * kkr, ones))
    kk = kkr / jnp.maximum(nrm, 1e-12)
    k2 = k * (1.0 + (a - 1.0) * ka_ref[...])
    r_out[...] = r
    w_out[...] = decay
    k_out[...] = k2
    v_out[...] = v
    kk_out[...] = kk
    b_out[...] = kk * a
    g_out[...] = g
    bonus_out[...] = _dot2(r * k2 * rk_ref[...], ones) * v


def rw_pre(z, prev, n_seq, tiles_per_seq, tt, blk0, lp, layer):
    nt = tiles_per_seq

    def zspec(width, col0):
        return pl.BlockSpec((tt, width), lambda s, j: (blk0 + s * nt + j, col0 // width))

    def pspec(width):
        return pl.BlockSpec((None, 1, width), lambda s, j: (s * nt + j, 0, 0))

    def lspec(width):
        return pl.BlockSpec((None, 1, width), lambda s, j: (layer, 0, 0))

    def wspec(rows_, cols_):
        return pl.BlockSpec((None, rows_, cols_), lambda s, j: (layer, 0, 0))

    tm_spec = pl.BlockSpec((tt, D_MIX), lambda s, j: (j, s))
    rm_spec = pl.BlockSpec((tt, D_MIX), lambda s, j: (s * nt + j, 0))
    tm_shape = jax.ShapeDtypeStruct((nt * tt, n_seq * D_MIX), F32)
    rm_shape = jax.ShapeDtypeStruct((n_seq * nt * tt, D_MIX), F32)
    return pl.pallas_call(
        _rw_pre_body,
        grid=(n_seq, nt),
        in_specs=[zspec(D_MIX, Z_RR), zspec(D_MIX, Z_RK), zspec(D_MIX, Z_RV), zspec(LORA_W, Z_LORA),
                  pspec(D_MIX), pspec(D_MIX), pspec(D_MIX), pspec(LORA_W),
                  lspec(D_MIX), lspec(D_MIX), lspec(D_MIX), lspec(LORA_W),
                  lspec(D_MIX), wspec(LANES, D_MIX), lspec(D_MIX), wspec(LANES, D_MIX), wspec(2 * LANES, D_MIX),
                  lspec(D_MIX), lspec(D_MIX), lspec(D_MIX)],
        out_specs=[tm_spec] * 6 + [rm_spec] * 2,
        out_shape=[tm_shape] * 6 + [rm_shape] * 2,
        compiler_params=_cparams(("parallel", "parallel")),
        name="rw_pre",
    )(z, z, z, z, prev["r"], prev["k"], prev["v"], prev["l"],
      lp["mu_r"], lp["mu_k"], lp["mu_v"], lp["mu_l"], lp["w0"], lp["w2"], lp["a0"], lp["a2"], lp["g2"],
      lp["kk"], lp["ka"], lp["rk"])


RW_GB = 4
RW_TILE = 256


def _rw_scan_body(r_ref, w_ref, k_ref, v_ref, kk_ref, b_ref, s0_ref, y_ref, s_out_ref, s_ref):
    tb = r_ref.shape[0]
    nj = D_MIX // RW_TILE
    hpt = RW_TILE // RW_HEAD

    @pl.when(pl.program_id(1) == 0)
    def _():
        s_ref[...] = s0_ref[...]

    ones4 = _seg_ones(RW_TILE, RW_HEAD)
    rr = lax.broadcasted_iota(jnp.int32, (RW_HEAD, RW_TILE), 0)
    cc = lax.broadcasted_iota(jnp.int32, (RW_HEAD, RW_TILE), 1)
    eye_rep = jnp.where(cc % RW_HEAD == rr, 1.0, 0.0)
    hr = lax.broadcasted_iota(jnp.int32, (SUBLANES, RW_TILE), 0)
    hc = lax.broadcasted_iota(jnp.int32, (SUBLANES, RW_TILE), 1)
    head_rows = jnp.where(hc // RW_HEAD == hr, 1.0, 0.0)
    tiles = [(bi, j) for bi in range(RW_GB) for j in range(nj)]

    def step(t, carry):
        def row(ref, bi, j):
            return ref[pl.ds(t, 1), D_MIX * bi + RW_TILE * j:D_MIX * bi + RW_TILE * (j + 1)]

        s_old, p_list, vd_list = [], [], []
        for bi, j in tiles:
            s = s_ref[bi, :, RW_TILE * j:RW_TILE * (j + 1)]
            s_old.append(s)
            p_list.append(s.astype(BF16).astype(F32) * row(kk_ref, bi, j).astype(BF16).astype(F32))
            vd_list.append(eye_rep * row(v_ref, bi, j))
        p_hi, p_lo = _split2(jnp.concatenate(p_list, axis=0))
        vd_hi, vd_lo = _split2(jnp.concatenate(vd_list, axis=0))
        lhs = jnp.concatenate([p_hi, p_lo, vd_hi, vd_lo], axis=0)
        res = jnp.dot(lhs, ones4, preferred_element_type=F32)
        n = len(tiles) * RW_HEAD
        sa_all = -(res[0:n] + res[n:2 * n])
        vcol_all = res[2 * n:3 * n] + res[3 * n:4 * n]
        for idx, (bi, j) in enumerate(tiles):
            sl = slice(RW_HEAD * idx, RW_HEAD * (idx + 1))
            s_new = (s_old[idx] * row(w_ref, bi, j) + sa_all[sl] * row(b_ref, bi, j)
                     + vcol_all[sl] * row(k_ref, bi, j))
            s_ref[bi, :, RW_TILE * j:RW_TILE * (j + 1)] = s_new
            y8 = _bdot_nt(head_rows * row(r_ref, bi, j), s_new)
            y_ref[t, RW_HEADS * bi + hpt * j:RW_HEADS * bi + hpt * (j + 1), :] = y8[0:hpt, :]
        return carry

    lax.fori_loop(0, tb, step, 0)

    @pl.when(pl.program_id(1) == pl.num_programs(1) - 1)
    def _():
        s_out_ref[...] = s_ref[...]


def rw_scan(seqs, s0, tb):
    steps, width = seqs[0].shape
    n_seq = width // D_MIX
    gw = RW_GB * D_MIX
    in_spec = pl.BlockSpec((tb, gw), lambda g, i: (i, g))
    st_spec = pl.BlockSpec((RW_GB, RW_HEAD, D_MIX), lambda g, i: (g, 0, 0))
    return pl.pallas_call(
        _rw_scan_body,
        grid=(n_seq // RW_GB, steps // tb),
        in_specs=[in_spec] * 6 + [st_spec],
        out_specs=[pl.BlockSpec((tb, RW_GB * RW_HEADS, RW_HEAD), lambda g, i: (i, g, 0)), st_spec],
        out_shape=[jax.ShapeDtypeStruct((steps, n_seq * RW_HEADS, RW_HEAD), F32),
                   jax.ShapeDtypeStruct((n_seq, RW_HEAD, D_MIX), F32)],
        scratch_shapes=[pltpu.VMEM((RW_GB, RW_HEAD, D_MIX), F32)],
        compiler_params=_cparams(("parallel", "arbitrary")),
        name="rw_scan",
    )(*seqs, s0)


def _rw_post_body(y_ref, g_ref, bonus_ref, lng_ref, lnb_ref, o_ref):
    y = y_ref[...]
    ones = _seg_ones(D_MIX, RW_HEAD)
    mu = _dot2(y, ones) * (1.0 / RW_HEAD)
    d = y - mu
    var = _dot2(d * d, ones) * (1.0 / RW_HEAD)
    yn = d * lax.rsqrt(var + RW_LN_EPS) * lng_ref[...] + lnb_ref[...]
    o_ref[...] = (yn + bonus_ref[...]) * g_ref[...]


def rw_post(y_tm, g, bonus, n_seq, tiles_per_seq, tt, lng3, lnb3, layer):
    nt = tiles_per_seq
    rm_spec = pl.BlockSpec((tt, D_MIX), lambda s, j: (s * nt + j, 0))
    lspec = pl.BlockSpec((None, 1, D_MIX), lambda s, j: (layer, 0, 0))
    return pl.pallas_call(
        _rw_post_body,
        grid=(n_seq, nt),
        in_specs=[pl.BlockSpec((tt, D_MIX), lambda s, j: (j, s)), rm_spec, rm_spec, lspec, lspec],
        out_specs=rm_spec,
        out_shape=jax.ShapeDtypeStruct((n_seq * nt * tt, D_MIX), F32),
        compiler_params=_cparams(("parallel", "parallel")),
        name="rw_post",
    )(y_tm, g, bonus, lng3, lnb3)


S5_W = S5_GROUPS * S5_STATE
S5_KT = 256
S5_NT = S5_KT // S5_GROUP * S5_STATE


def _s5_body(u_ref, bbr_ref, bbi_ref, ccr_ref, cci_ref, d_ref, a_ref, apow_ref, gw_ref, gb_ref, h0r_ref, h0i_ref,
             o_ref, hr_out, hi_out, xr_ref, xi_ref, cr_ref, ci_ref):
    tt = u_ref.shape[0]
    nkt = D_MIX // S5_KT

    @pl.when(pl.program_id(1) == 0)
    def _():
        cr_ref[...] = h0r_ref[...]
        ci_ref[...] = h0i_ref[...]

    u = u_ref[...]
    for jt in range(nkt):
        ub = u[:, S5_KT * jt:S5_KT * (jt + 1)]
        xr_ref[:, S5_NT * jt:S5_NT * (jt + 1)] = _bdot(ub, bbr_ref[jt])
        xi_ref[:, S5_NT * jt:S5_NT * (jt + 1)] = _bdot(ub, bbi_ref[jt])

    sub = lax.broadcasted_iota(jnp.int32, (SUBLANES, S5_W), 0)
    a_re = [a_ref[2 * i:2 * i + 1, :] for i in range(3)]
    a_im = [a_ref[2 * i + 1:2 * i + 2, :] for i in range(3)]
    pw_re, pw_im = apow_ref[0], apow_ref[1]

    def group(gi, carry):
        r0 = pl.multiple_of(gi * SUBLANES, SUBLANES)
        hr, hi = xr_ref[pl.ds(r0, SUBLANES), :], xi_ref[pl.ds(r0, SUBLANES), :]
        for i, sh in enumerate((1, 2, 4)):
            keep = sub >= sh
            sr = jnp.where(keep, pltpu.roll(hr, sh, 0), 0.0)
            si = jnp.where(keep, pltpu.roll(hi, sh, 0), 0.0)
            hr, hi = hr + a_re[i] * sr - a_im[i] * si, hi + a_re[i] * si + a_im[i] * sr
        c_r, c_i = cr_ref[...], ci_ref[...]
        hr = hr + pw_re * c_r - pw_im * c_i
        hi = hi + pw_re * c_i + pw_im * c_r
        xr_ref[pl.ds(r0, SUBLANES), :] = hr
        xi_ref[pl.ds(r0, SUBLANES), :] = hi
        cr_ref[...] = hr[SUBLANES - 1:SUBLANES, :]
        ci_ref[...] = hi[SUBLANES - 1:SUBLANES, :]
        return carry

    lax.fori_loop(0, tt // SUBLANES, group, 0)

    ys = []
    for jt in range(nkt):
        hr = xr_ref[:, S5_NT * jt:S5_NT * (jt + 1)]
        hi = xi_ref[:, S5_NT * jt:S5_NT * (jt + 1)]
        ys.append(_bdot(hr, ccr_ref[jt]) - _bdot(hi, cci_ref[jt]))
    y = jnp.concatenate(ys, axis=1) + d_ref[...] * u
    ya = 0.5 * y * (1.0 + jnp.tanh(math.sqrt(2.0 / math.pi) * (y + 0.044715 * (y * y * y))))
    o_ref[...] = ya * _sigmoid(_bdot(ya, gw_ref[...]) + gb_ref[...])

    @pl.when(pl.program_id(1) == pl.num_programs(1) - 1)
    def _():
        hr_out[...] = cr_ref[...]
        hi_out[...] = ci_ref[...]


def s5(z, blk0, n_seq, tiles_per_seq, tt, sp, layer, h0r, h0i):
    nt = tiles_per_seq
    nkt = D_MIX // S5_KT

    def const3(shape):
        return pl.BlockSpec(shape, lambda s, j: (0,) * len(shape))

    st_spec = pl.BlockSpec((None, 1, S5_W), lambda s, j: (s, 0, 0))
    return pl.pallas_call(
        _s5_body,
        grid=(n_seq, nt),
        in_specs=[pl.BlockSpec((tt, D_MIX), lambda s, j: (blk0 + s * nt + j, Z_S5 // D_MIX)),
                  const3((nkt, S5_KT, S5_NT)), const3((nkt, S5_KT, S5_NT)),
                  const3((nkt, S5_NT, S5_KT)), const3((nkt, S5_NT, S5_KT)),
                  const3((1, D_MIX)), const3((6, S5_W)), const3((2, SUBLANES, S5_W)),
                  pl.BlockSpec((None, D_MIX, D_MIX), lambda s, j: (layer, 0, 0)),
                  pl.BlockSpec((None, 1, D_MIX), lambda s, j: (layer, 0, 0)),
                  st_spec, st_spec],
        out_specs=[pl.BlockSpec((tt, D_MIX), lambda s, j: (s * nt + j, 0)), st_spec, st_spec],
        out_shape=[jax.ShapeDtypeStruct((n_seq * nt * tt, D_MIX), F32),
                   jax.ShapeDtypeStruct((n_seq, 1, S5_W), F32),
                   jax.ShapeDtypeStruct((n_seq, 1, S5_W), F32)],
        scratch_shapes=[pltpu.VMEM((tt, S5_W), F32), pltpu.VMEM((tt, S5_W), F32),
                        pltpu.VMEM((1, S5_W), F32), pltpu.VMEM((1, S5_W), F32)],
        compiler_params=_cparams(("parallel", "arbitrary")),
        name="s5",
    )(z, sp["bbr"], sp["bbi"], sp["ccr"], sp["cci"], sp["d"], sp["a"], sp["apow"], sp["glu_w"], sp["glu_b"], h0r, h0i)


def _s5_tables(lam_re, lam_im, log_step, b_re, b_im, c_re, c_im, d):
    dt = jnp.exp(log_step.astype(F32))[:, None]
    lr = jnp.minimum(lam_re.astype(F32), -1e-4)
    li = lam_im.astype(F32)
    mag = jnp.exp(lr * dt)
    ab_re, ab_im = mag * jnp.cos(li * dt), mag * jnp.sin(li * dt)
    den = lr * lr + li * li
    f_re = ((ab_re - 1.0) * lr + ab_im * li) / den
    f_im = (ab_im * lr - (ab_re - 1.0) * li) / den
    bb_re = f_re[..., None] * b_re - f_im[..., None] * b_im
    bb_im = f_re[..., None] * b_im + f_im[..., None] * b_re
    gpt = S5_KT // S5_GROUP
    nkt = S5_GROUPS // gpt
    eye = jnp.eye(gpt, dtype=F32)

    def in_tiles(bb):
        t = bb.reshape(nkt, gpt, S5_STATE, S5_GROUP)
        t = jnp.einsum('jgph,gk->jghkp', t, eye)
        return t.reshape(nkt, S5_KT, S5_NT)

    def out_tiles(cc):
        t = cc.reshape(nkt, gpt, S5_GROUP, S5_STATE)
        t = jnp.einsum('jghp,gk->jgpkh', t, eye)
        return t.reshape(nkt, S5_NT, S5_KT)

    ar, ai = ab_re.reshape(1, S5_W), ab_im.reshape(1, S5_W)
    pows_r, pows_i = [ar], [ai]
    for _ in range(SUBLANES - 1):
        pr, pi = pows_r[-1], pows_i[-1]
        pows_r.append(pr * ar - pi * ai)
        pows_i.append(pr * ai + pi * ar)
    a_tab = jnp.concatenate([pows_r[0], pows_i[0], pows_r[1], pows_i[1], pows_r[3], pows_i[3]], axis=0)
    apow = jnp.stack([jnp.concatenate(pows_r, axis=0), jnp.concatenate(pows_i, axis=0)])
    return {"bbr": in_tiles(bb_re).astype(BF16), "bbi": in_tiles(bb_im).astype(BF16),
            "ccr": out_tiles(c_re.astype(F32)).astype(BF16), "cci": out_tiles(c_im.astype(F32)).astype(BF16), "d": d.reshape(1, D_MIX).astype(F32), "a": a_tab, "apow": apow}


def _merge_body(o0, o1, o2, o3, wb_ref, g0, g1, g2, g3, out_ref):
    acc = None
    for i, (o_ref, g_ref) in enumerate(((o0, g0), (o1, g1), (o2, g2), (o3, g3))):
        term = _sigmoid(g_ref[...]) * _bdot(o_ref[...], wb_ref[i])
        acc = term if acc is None else acc + term
    out_ref[...] = acc.astype(out_ref.dtype)


def merge(outs, w_branch, zgate, layer, *, tm, tn):
    m = outs[0].shape[0]
    nj = D_MODEL // tn
    o_spec = pl.BlockSpec((tm, D_MIX), lambda i, j: (i, 0))

    def gspec(b):
        return pl.BlockSpec((tm, tn), lambda i, j: (i, b * nj + j))

    return pl.pallas_call(
        _merge_body,
        grid=(m // tm, nj),
        in_specs=[o_spec] * 4 + [pl.BlockSpec((None, N_BRANCH, D_MIX, tn), lambda i, j: (layer, 0, 0, j))]
        + [gspec(b) for b in range(N_BRANCH)],
        out_specs=pl.BlockSpec((tm, tn), lambda i, j: (i, j)),
        out_shape=jax.ShapeDtypeStruct((m, D_MODEL), BF16),
        compiler_params=_cparams(("parallel", "arbitrary")),
        name="merge",
    )(*outs, w_branch, zgate, zgate, zgate, zgate)


def _router_body(x_ref, g_ref, wr_ref, br_ref, xn_ref, route_ref):
    x = x_ref[...]
    xn = x * lax.rsqrt(jnp.mean(x * x, axis=-1, keepdims=True) + EPS) * g_ref[...]
    xn_ref[...] = xn
    logits = _bdot(xn, wr_ref[...]) + br_ref[...]
    lane = lax.broadcasted_iota(jnp.int32, logits.shape, 1)
    big = jnp.int32(1 << 20)
    ninf = -jnp.inf
    lg = jnp.where(lane < N_GROUPS, logits, ninf)
    gmax = jnp.max(lg, axis=-1, keepdims=True)
    g_top = jnp.min(jnp.where(lg == gmax, lane, big), axis=-1, keepdims=True)
    pg = 1.0 / jnp.sum(jnp.exp(lg - gmax), axis=-1, keepdims=True)
    e_lane = lane - N_GROUPS
    in_group = (e_lane >= 0) & (e_lane < N_EXPERTS) & ((e_lane // EXP_PER_GROUP) == g_top)
    le = jnp.where(in_group, logits, ninf)
    m1 = jnp.max(le, axis=-1, keepdims=True)
    i1 = jnp.min(jnp.where(le == m1, lane, big), axis=-1, keepdims=True)
    le2 = jnp.where(lane == i1, ninf, le)
    m2 = jnp.max(le2, axis=-1, keepdims=True)
    i2 = jnp.min(jnp.where(le2 == m2, lane, big), axis=-1, keepdims=True)
    e2 = jnp.exp(m2 - m1)
    w1 = pg / (1.0 + e2)
    w2 = pg * e2 / (1.0 + e2)
    out = jnp.where(lane == 0, (i1 - N_GROUPS).astype(F32),
                    jnp.where(lane == 1, (i2 - N_GROUPS).astype(F32),
                              jnp.where(lane == 2, w1, jnp.where(lane == 3, w2, 0.0))))
    route_ref[...] = out


def router(x, g3, wr3, br3, layer, tm):
    m, d = x.shape
    return pl.pallas_call(
        _router_body,
        grid=(m // tm,),
        in_specs=[pl.BlockSpec((tm, d), lambda i: (i, 0)),
                  pl.BlockSpec((None, 1, d), lambda i: (layer, 0, 0)),
                  pl.BlockSpec((None, d, LANES), lambda i: (layer, 0, 0)),
                  pl.BlockSpec((None, 1, LANES), lambda i: (layer, 0, 0))],
        out_specs=[pl.BlockSpec((tm, d), lambda i: (i, 0)), pl.BlockSpec((tm, LANES), lambda i: (i, 0))],
        out_shape=[jax.ShapeDtypeStruct((m, d), F32), jax.ShapeDtypeStruct((m, LANES), F32)],
        compiler_params=_cparams(("parallel",)),
        name="moe_router",
    )(x, g3, wr3, br3)


def _gather_rows(src_hbm, idx_ref, base, dst, sem, n):
    for r in range(n):
        pltpu.make_async_copy(src_hbm.at[pl.ds(idx_ref[base + r], 1), :], dst.at[pl.ds(r, 1), :], sem).start()


def _wait_rows(src_hbm, dst, sem, n):
    for r in range(n):
        pltpu.make_async_copy(src_hbm.at[pl.ds(0, 1), :], dst.at[pl.ds(r, 1), :], sem).wait()


def _moe_up_body(exp_ref, tok_ref, x_hbm, w1_ref, w3_ref, h_ref, xbuf, sems):
    i = pl.program_id(0)
    nblk = pl.num_programs(0)
    slot = i % 2

    @pl.when(i == 0)
    def _():
        _gather_rows(x_hbm, tok_ref, 0, xbuf.at[0], sems.at[0], MOE_BLOCK)

    @pl.when(i + 1 < nblk)
    def _():
        _gather_rows(x_hbm, tok_ref, (i + 1) * MOE_BLOCK, xbuf.at[1 - slot], sems.at[1 - slot], MOE_BLOCK)

    _wait_rows(x_hbm, xbuf.at[slot], sems.at[slot], MOE_BLOCK)
    xb = xbuf[slot].astype(BF16)
    a = jnp.dot(xb, w1_ref[...].astype(BF16), preferred_element_type=F32)
    b = jnp.dot(xb, w3_ref[...].astype(BF16), preferred_element_type=F32)
    h_ref[...] = _silu(a) * b


def moe_up(blk_exp, slot_tok, xn, w1, w3, layer):
    n_blocks = blk_exp.shape[0]
    d = xn.shape[1]
    grid_spec = pltpu.PrefetchScalarGridSpec(
        num_scalar_prefetch=2,
        grid=(n_blocks,),
        in_specs=[pl.BlockSpec(memory_space=pl.ANY),
                  pl.BlockSpec((None, None, d, D_EXPERT), lambda i, e, t: (layer, e[i], 0, 0)),
                  pl.BlockSpec((None, None, d, D_EXPERT), lambda i, e, t: (layer, e[i], 0, 0))],
        out_specs=pl.BlockSpec((MOE_BLOCK, D_EXPERT), lambda i, e, t: (i, 0)),
        scratch_shapes=[pltpu.VMEM((2, MOE_BLOCK, d), F32), pltpu.SemaphoreType.DMA((2,))],
    )
    return pl.pallas_call(
        _moe_up_body,
        grid_spec=grid_spec,
        out_shape=jax.ShapeDtypeStruct((n_blocks * MOE_BLOCK, D_EXPERT), F32),
        compiler_params=_cparams(("arbitrary",)),
        name="moe_up",
    )(blk_exp, slot_tok, xn, w1, w3)


def _moe_down_body(exp_ref, h_ref, w2_ref, sw_ref, y_ref):
    y_ref[...] = jnp.dot(h_ref[...].astype(BF16), w2_ref[...].astype(BF16), preferred_element_type=F32) * sw_ref[...]


def moe_down(blk_exp, h, w2, slot_w, layer):
    n_blocks = blk_exp.shape[0]
    d = w2.shape[3]
    grid_spec = pltpu.PrefetchScalarGridSpec(
        num_scalar_prefetch=1,
        grid=(n_blocks,),
        in_specs=[pl.BlockSpec((MOE_BLOCK, D_EXPERT), lambda i, e: (i, 0)),
                  pl.BlockSpec((None, None, D_EXPERT, d), lambda i, e: (layer, e[i], 0, 0)),
                  pl.BlockSpec((MOE_BLOCK, 1), lambda i, e: (i, 0))],
        out_specs=pl.BlockSpec((MOE_BLOCK, d), lambda i, e: (i, 0)),
    )
    return pl.pallas_call(
        _moe_down_body,
        grid_spec=grid_spec,
        out_shape=jax.ShapeDtypeStruct((n_blocks * MOE_BLOCK, d), F32),
        compiler_params=_cparams(("arbitrary",)),
        name="moe_down",
    )(blk_exp, h, w2, slot_w)


MOE_TC = 64


def _moe_combine_body(slot_ref, x_ref, y_hbm, o_ref, ybuf, sems):
    i = pl.program_id(0)
    nstep = pl.num_programs(0)
    slot = i % 2
    n = TOP_K * MOE_TC

    @pl.when(i == 0)
    def _():
        _gather_rows(y_hbm, slot_ref, 0, ybuf.at[0], sems.at[0], n)

    @pl.when(i + 1 < nstep)
    def _():
        _gather_rows(y_hbm, slot_ref, (i + 1) * n, ybuf.at[1 - slot], sems.at[1 - slot], n)

    _wait_rows(y_hbm, ybuf.at[slot], sems.at[slot], n)
    o_ref[...] = x_ref[...] + (ybuf[slot, 0:MOE_TC, :] + ybuf[slot, MOE_TC:n, :])


def moe_combine(slot_of, x, yb):
    m, d = x.shape
    grid_spec = pltpu.PrefetchScalarGridSpec(
        num_scalar_prefetch=1,
        grid=(m // MOE_TC,),
        in_specs=[pl.BlockSpec((MOE_TC, d), lambda i, s: (i, 0)),
                  pl.BlockSpec(memory_space=pl.ANY)],
        out_specs=pl.BlockSpec((MOE_TC, d), lambda i, s: (i, 0)),
        scratch_shapes=[pltpu.VMEM((2, TOP_K * MOE_TC, d), F32), pltpu.SemaphoreType.DMA((2,))],
    )
    return pl.pallas_call(
        _moe_combine_body,
        grid_spec=grid_spec,
        out_shape=jax.ShapeDtypeStruct((m, d), F32),
        compiler_params=_cparams(("arbitrary",)),
        name="moe_combine",
    )(slot_of, x, yb)


def _moe_plan(route, n_tok):
    e_idx = route[:, 0:TOP_K].astype(jnp.int32)
    w = route[:, TOP_K:2 * TOP_K]
    n_assign = n_tok * TOP_K
    e_flat = e_idx.reshape(n_assign)
    onehot = (e_flat[:, None] == jnp.arange(N_EXPERTS, dtype=jnp.int32)[None, :]).astype(jnp.int32)
    csum = jnp.cumsum(onehot, axis=0)
    counts = csum[-1]
    rank = jnp.sum((csum - onehot) * onehot, axis=1)
    padded = (counts + MOE_BLOCK - 1) // MOE_BLOCK * MOE_BLOCK
    pad_end = jnp.cumsum(padded)
    pad_start = pad_end - padded
    dest = pad_start[e_flat] + rank
    n_blocks = -(-(n_assign + N_EXPERTS * (MOE_BLOCK - 1)) // MOE_BLOCK)
    n_slots = n_blocks * MOE_BLOCK
    tok_flat = jnp.repeat(jnp.arange(n_tok, dtype=jnp.int32), TOP_K)
    slot_tok = jnp.zeros((n_slots,), jnp.int32).at[dest].set(tok_flat)
    slot_w = jnp.zeros((n_slots,), F32).at[dest].set(w.reshape(n_assign))
    blk_exp = jnp.minimum(
        jnp.searchsorted(pad_end, jnp.arange(n_blocks, dtype=jnp.int32) * MOE_BLOCK, side='right'),
        N_EXPERTS - 1).astype(jnp.int32)
    d2 = dest.reshape(n_tok // MOE_TC, MOE_TC, TOP_K)
    slot_of = jnp.transpose(d2, (0, 2, 1)).reshape(-1)
    return blk_exp, slot_tok, slot_w.reshape(n_slots, 1), slot_of


def kernel(x_prompt, x_sample, cache_attn_meta_k, cache_attn_meta_v, cache_attn_win_k, cache_attn_win_v, state_gla, state_rwkv, state_rwkv_shift, state_s5_re, state_s5_im, meta_tokens, norm_mix, w_in, w_gla_gate, b_gla_gate, gla_norm, attn_sinks, rw_mu, rw_w0, rw_w2, rw_a0, rw_a2, rw_g2, rw_kk, rw_ka, rw_rk, rw_ln_g, rw_ln_b, s5_lam_re, s5_lam_im, s5_log_step, s5_b_re, s5_b_im, s5_c_re, s5_c_im, s5_d, s5_glu_w, s5_glu_b, w_branch, w_out, norm_moe, w_router_group, b_router_group, w_router_expert, b_router_expert, w_exp_gate, w_exp_up, w_exp_down, norm_final):
    depth = w_in.shape[0]
    nb, seq_p = x_prompt.shape[0], x_prompt.shape[1]
    db, t_len = x_sample.shape[0], x_sample.shape[1]
    seq = seq_p + N_META
    rows_p = nb * seq
    rows_s = db * t_len
    n_tok = rows_p + rows_s
    wb_len = cache_attn_win_k.shape[2]

    tm_big = n_tok // 7
    tm_mid = n_tok // 14
    tm_small = n_tok // 19
    rw_nt = 6
    rw_tt = seq // rw_nt
    rw_tb = seq // 43
    s5_tt = seq // 6

    meta = jnp.broadcast_to(meta_tokens.astype(F32)[None], (nb, N_META, D_MODEL))
    x = jnp.concatenate([jnp.concatenate([meta, x_prompt], axis=1).reshape(rows_p, D_MODEL),
                         x_sample.reshape(rows_s, D_MODEL)], axis=0)

    cos_p, sin_p = _rope_tables(jnp.arange(seq, dtype=jnp.int32))
    cos_s, sin_s = _rope_tables(PAST_LEN + jnp.arange(t_len, dtype=jnp.int32))

    def r3(a):
        return a.reshape(depth, 1, a.shape[-1]).astype(F32)

    norm_mix3, norm_moe3 = r3(norm_mix), r3(norm_moe)
    b_gla3, gla_norm3 = r3(b_gla_gate), r3(gla_norm)
    w2p = jnp.pad(w_gla_gate, ((0, 0), (0, LANES - GLA_RANK), (0, 0)))

    def lora_cols(a):
        c = 3 * D_MIX
        zw, za, zg = a[..., c:c + 64], a[..., c + 64:c + 128], a[..., c + 128:c + 288]
        pad = lambda t, n: jnp.pad(t, [(0, 0)] * (t.ndim - 1) + [(0, n - t.shape[-1])])
        lo = jnp.concatenate([pad(zw, LANES), pad(za, LANES), pad(zg, 2 * LANES)], axis=-1)
        return a[..., 0:D_MIX], a[..., D_MIX:2 * D_MIX], a[..., 2 * D_MIX:c], lo

    mu_r, mu_k, mu_v, mu_l = lora_cols(rw_mu)
    rw_params = {
        "mu_r": r3(mu_r), "mu_k": r3(mu_k), "mu_v": r3(mu_v), "mu_l": r3(mu_l),
        "w0": r3(rw_w0), "a0": r3(rw_a0), "kk": r3(rw_kk), "ka": r3(rw_ka),
        "rk": r3(rw_rk.reshape(depth, D_MIX)),
        "w2": jnp.pad(rw_w2, ((0, 0), (0, LANES - RW_DECAY_RANK), (0, 0))),
        "a2": jnp.pad(rw_a2, ((0, 0), (0, LANES - RW_A_RANK), (0, 0))),
        "g2": jnp.pad(rw_g2, ((0, 0), (0, 2 * LANES - RW_G_RANK), (0, 0))),
    }
    ln_g3, ln_b3 = r3(rw_ln_g), r3(rw_ln_b)
    glu_b3 = r3(s5_glu_b)
    wr3 = jnp.pad(jnp.concatenate([w_router_group, w_router_expert], axis=2).astype(F32),
                  ((0, 0), (0, 0), (0, LANES - N_GROUPS - N_EXPERTS)))
    br3 = jnp.pad(jnp.concatenate([b_router_group, b_router_expert], axis=1).astype(F32),
                  ((0, 0), (0, LANES - N_GROUPS - N_EXPERTS))).reshape(depth, 1, LANES)

    rec = [[] for _ in range(16)]
    zeros_gla = jnp.zeros((nb, GLA_HEADS, GLA_DK, GLA_DV), F32)
    zeros_rw = jnp.zeros((nb, RW_HEAD, D_MIX), F32)
    zeros_s5 = jnp.zeros((nb, 1, S5_W), F32)

    for l in range(depth):
        wl = w_in[l]
        zcol = lambda n: jnp.zeros((D_MODEL, n), F32)
        rw0 = _SRC_RW
        w_mix = jnp.concatenate([
            wl[:, _SRC_Q:_SRC_Q + 1024],
            wl[:, rw0:rw0 + 3072],
            wl[:, _SRC_S5:_SRC_S5 + 1024],
            wl[:, _SRC_GV:_SRC_GV + 1024], wl[:, _SRC_GR:_SRC_GR + 1024],
            wl[:, rw0 + 3072:rw0 + 3136], zcol(64), wl[:, rw0 + 3136:rw0 + 3200], zcol(64),
            wl[:, rw0 + 3200:rw0 + 3360], zcol(96),
            wl[:, _SRC_GQ:_SRC_GQ + 512], wl[:, _SRC_GK:_SRC_GK + 512],
            wl[:, _SRC_K:_SRC_K + 128], wl[:, _SRC_V:_SRC_V + 128],
            wl[:, _SRC_GLR:_SRC_GLR + 16], zcol(112), zcol(NZ_MIX - Z_GLR - LANES)], axis=1)[None]
        w_gate = wl[:, _SRC_GATE:][None]
        xn = rmsnorm_rows(x, norm_mix3, l, BF16, tm_small)
        z = matmul_rows(xn, w_mix, 0, tm=tm_big, tn=512, name="in_proj_mix")
        zgate = matmul_rows(xn, w_gate, 0, tm=tm_big, tn=512, name="in_proj_gate")

        sinks = attn_sinks[l].astype(F32)
        o_att_p, kr_p = attn_prompt(z, sinks, cos_p, sin_p, nb, seq)
        mk = cache_attn_meta_k[l].reshape(db, N_META, LANES)
        mv = cache_attn_meta_v[l].reshape(db, N_META, LANES)
        wk = cache_attn_win_k[l].reshape(db, wb_len, LANES)
        wv = cache_attn_win_v[l].reshape(db, wb_len, LANES)
        o_att_s, kr_s = attn_sample(z, rows_p, sinks, cos_s, sin_s, mk, mv, wk, wv, db, t_len)
        v_p = z[:rows_p, Z_AV:Z_AV + LANES].reshape(nb, seq, N_KV, HEAD_DIM)
        k_p = kr_p.reshape(nb, seq, N_KV, HEAD_DIM)
        v_s = z[rows_p:, Z_AV:Z_AV + LANES].reshape(db, t_len, N_KV, HEAD_DIM)
        k_s = kr_s.reshape(db, t_len, N_KV, HEAD_DIM)
        new_wk = jnp.concatenate([cache_attn_win_k[l].astype(F32), k_s], axis=1)[:, -wb_len:]
        new_wv = jnp.concatenate([cache_attn_win_v[l].astype(F32), v_s], axis=1)[:, -wb_len:]

        o_gla_p, gla_p = gla(z, 0, nb, seq, w2p, b_gla3, gla_norm3, l, zeros_gla)
        o_gla_s, gla_s = gla(z, rows_p // t_len, db, t_len, w2p, b_gla3, gla_norm3, l, state_gla[l].astype(F32))

        def seg_rows(idx):
            return {"r": z[idx, Z_RR:Z_RR + D_MIX][:, None], "k": z[idx, Z_RK:Z_RK + D_MIX][:, None],
                    "v": z[idx, Z_RV:Z_RV + D_MIX][:, None], "l": z[idx, Z_LORA:Z_LORA + LORA_W][:, None]}

        tile_start = (jnp.arange(nb, dtype=jnp.int32)[:, None] * seq
                      + jnp.arange(rw_nt, dtype=jnp.int32)[None, :] * rw_tt).reshape(-1)
        prev_p = seg_rows(jnp.maximum(tile_start - 1, 0))
        is_first = (jnp.arange(nb * rw_nt) % rw_nt == 0)[:, None, None]
        prev_p = {kq: jnp.where(is_first, 0.0, vq) for kq, vq in prev_p.items()}
        sr, sk, sv, sl = lora_cols(state_rwkv_shift[l].astype(F32))
        prev_s = {"r": sr[:, None], "k": sk[:, None], "v": sv[:, None], "l": sl[:, None]}

        pre_p = rw_pre(z, prev_p, nb, rw_nt, rw_tt, 0, rw_params, l)
        pre_s = rw_pre(z, prev_s, db, 1, t_len, rows_p // t_len, rw_params, l)
        y_p, st_p = rw_scan(pre_p[:6], zeros_rw, rw_tb)
        s0_s = jnp.transpose(state_rwkv[l].astype(F32), (0, 2, 1, 3)).reshape(db, RW_HEAD, D_MIX)
        y_s, st_s = rw_scan(pre_s[:6], s0_s, t_len)
        o_rw_p = rw_post(y_p.reshape(seq, nb * D_MIX), pre_p[6], pre_p[7], nb, rw_nt, rw_tt, ln_g3, ln_b3, l)
        o_rw_s = rw_post(y_s.reshape(t_len, db * D_MIX), pre_s[6], pre_s[7], db, 1, t_len, ln_g3, ln_b3, l)

        def rw_state(st, n):
            return jnp.transpose(st.reshape(n, RW_HEAD, RW_HEADS, RW_HEAD), (0, 2, 1, 3))

        def shift_out(last_rows):
            zl = z[last_rows]
            lo = zl[:, Z_LORA:Z_LORA + LORA_W]
            return jnp.concatenate([zl[:, Z_RR:Z_RR + 3 * D_MIX], lo[:, 0:64], lo[:, 128:192], lo[:, 256:416]], axis=1)

        sh_p = shift_out(jnp.arange(nb, dtype=jnp.int32) * seq + seq - 1)
        sh_s = shift_out(rows_p + jnp.arange(db, dtype=jnp.int32) * t_len + t_len - 1)

        s5p = _s5_tables(s5_lam_re[l], s5_lam_im[l], s5_log_step[l], s5_b_re[l], s5_b_im[l], s5_c_re[l], s5_c_im[l],
                         s5_d[l])
        s5p["glu_w"] = s5_glu_w
        s5p["glu_b"] = glu_b3
        o_s5_p, re_p, im_p = s5(z, 0, nb, 6, s5_tt, s5p, l, zeros_s5, zeros_s5)
        o_s5_s, re_s, im_s = s5(z, rows_p // t_len, db, 1, t_len, s5p, l,
                                state_s5_re[l].astype(F32).reshape(db, 1, S5_W),
                                state_s5_im[l].astype(F32).reshape(db, 1, S5_W))

        outs = [jnp.concatenate([a, b], axis=0) for a, b in
                ((o_att_p, o_att_s), (o_gla_p, o_gla_s), (o_rw_p, o_rw_s), (o_s5_p, o_s5_s))]
        merged = merge(outs, w_branch, zgate, l, tm=tm_mid, tn=512)
        x = matmul_rows(merged, w_out, l, tm=tm_big, tn=512, residual=x, name="out_proj")

        xn2, route = router(x, norm_moe3, wr3, br3, l, tm_small)
        blk_exp, slot_tok, slot_w, slot_of = _moe_plan(route, n_tok)
        hmid = moe_up(blk_exp, slot_tok, xn2, w_exp_gate, w_exp_up, l)
        yb = moe_down(blk_exp, hmid, w_exp_down, slot_w, l)
        x = moe_combine(slot_of, x, yb)

        vals = (k_p[:, :N_META], v_p[:, :N_META], k_p[:, -WINDOW:], v_p[:, -WINDOW:],
                gla_p, rw_state(st_p, nb), sh_p, re_p.reshape(nb, S5_GROUPS, S5_STATE),
                im_p.reshape(nb, S5_GROUPS, S5_STATE),
                new_wk, new_wv, gla_s, rw_state(st_s, db), sh_s,
                re_s.reshape(db, S5_GROUPS, S5_STATE), im_s.reshape(db, S5_GROUPS, S5_STATE))
        for i in range(16):
            rec[i].append(vals[i])

    y = rmsnorm_rows(x, norm_final.reshape(1, 1, D_MODEL).astype(F32), 0, F32, tm_small)
    y_prompt = y[:rows_p].reshape(nb, seq, D_MODEL)[:, N_META:]
    y_sample = y[rows_p:].reshape(db, t_len, D_MODEL)
    return (y_prompt, y_sample, *[jnp.stack(r) for r in rec])
```

```python
import functools
import math

import numpy as np
import jax
import jax.numpy as jnp
from jax import lax
from jax.experimental import pallas as pl
from jax.experimental.pallas import tpu as pltpu

F32 = jnp.float32
BF16 = jnp.bfloat16

D_MODEL = 4096
N_META = 16
EPS = 1e-5
D_MIX = 1024
HEAD_DIM = 64
N_HEADS = 16
N_KV = 2
WINDOW = 128
ROPE_DIM = 16
ROPE_THETA = 500000.0
NEG_INF = -1e30
GLA_HEADS = 4
GLA_DK = 128
GLA_DV = 256
GLA_RANK = 16
GLA_NORMALIZER = 16.0
GLA_CHUNK = 64
RW_HEAD = 64
RW_HEADS = 16
RW_DECAY_RANK = 64
RW_A_RANK = 64
RW_G_RANK = 160
RW_IN = 3 * D_MIX + RW_DECAY_RANK + RW_A_RANK + RW_G_RANK
RW_LN_EPS = 64e-5
S5_GROUP = 16
S5_GROUPS = 64
S5_STATE = 64
N_GROUPS = 4
EXP_PER_GROUP = 8
N_EXPERTS = 32
TOP_K = 2
D_EXPERT = 512
MOE_BLOCK = 128
PAST_LEN = 16384
N_BRANCH = 4

LANES = 128
SUBLANES = 8
VMEM_BUDGET = 56 * 1024 * 1024

Z_Q = 0
Z_AK = 1024
Z_AV = 1152
Z_GQ = 1280
Z_GK = 1792
Z_GV = 2304
Z_GR = 3328
NZ_1 = 4352
Z_RR = 0
Z_RK = 1024
Z_RV = 2048
Z_S5 = 3072
Z_LORA = 4096
Z_GLR = 4608
NZ_2 = 5120
LORA_W = 512

_SRC_Q, _SRC_K, _SRC_V, _SRC_GQ, _SRC_GK, _SRC_GV, _SRC_GR, _SRC_GLR = 0, 1024, 1152, 1280, 1792, 2304, 3328, 4352
_SRC_RW = 4368
_SRC_S5 = _SRC_RW + RW_IN
_SRC_GATE = _SRC_S5 + D_MIX


def _cparams(sem, vmem=VMEM_BUDGET):
    return pltpu.CompilerParams(dimension_semantics=sem, vmem_limit_bytes=int(vmem))


def _bdot(a, b):
    return jnp.dot(a.astype(BF16), b.astype(BF16), preferred_element_type=F32)


def _bdot_nt(a, b):
    return lax.dot_general(a.astype(BF16), b.astype(BF16), (((1,), (1,)), ((), ())), preferred_element_type=F32)


def _bdot_tn(a, b):
    return lax.dot_general(a.astype(BF16), b.astype(BF16), (((0,), (0,)), ((), ())), preferred_element_type=F32)


def _split2(x):
    hi = x.astype(BF16)
    lo = (x - hi.astype(F32)).astype(BF16)
    return hi, lo


def _dot2(x, w_bf16):
    hi, lo = _split2(x)
    return (jnp.dot(hi, w_bf16, preferred_element_type=F32) + jnp.dot(lo, w_bf16, preferred_element_type=F32))


def _sigmoid(x):
    return 1.0 / (1.0 + jnp.exp(-x))


def _silu(x):
    return x * _sigmoid(x)


def _softplus(x):
    return jnp.maximum(x, 0.0) + jnp.log(1.0 + jnp.exp(-jnp.abs(x)))


def _rmsnorm_body(x_ref, g_ref, o_ref):
    x = x_ref[...]
    y = x * lax.rsqrt(jnp.mean(x * x, axis=-1, keepdims=True) + EPS)
    o_ref[...] = (y * g_ref[...]).astype(o_ref.dtype)


def rmsnorm_rows(x, g3, layer, out_dtype, tm):
    m, d = x.shape
    return pl.pallas_call(
        _rmsnorm_body,
        grid=(m // tm,),
        in_specs=[pl.BlockSpec((tm, d), lambda i: (i, 0)),
                  pl.BlockSpec((None, 1, d), lambda i: (layer, 0, 0))],
        out_specs=pl.BlockSpec((tm, d), lambda i: (i, 0)),
        out_shape=jax.ShapeDtypeStruct((m, d), out_dtype),
        compiler_params=_cparams(("parallel",)),
        name="rmsnorm_rows",
    )(x, g3)


def _mm_body(a_ref, w_ref, o_ref):
    o_ref[...] = jnp.dot(a_ref[...], w_ref[...].astype(BF16), preferred_element_type=F32).astype(o_ref.dtype)


def _mm_res_body(a_ref, w_ref, r_ref, o_ref):
    o_ref[...] = r_ref[...] + jnp.dot(a_ref[...], w_ref[...].astype(BF16), preferred_element_type=F32)


def matmul_rows(a, w3, layer, *, tm, tn, n=None, residual=None, name="matmul_rows"):
    m, k = a.shape
    n = w3.shape[2] if n is None else n
    in_specs = [pl.BlockSpec((tm, k), lambda i, j: (i, 0)),
                pl.BlockSpec((None, k, tn), lambda i, j: (layer, 0, j))]
    args = [a, w3]
    body = _mm_body
    if residual is not None:
        in_specs.append(pl.BlockSpec((tm, tn), lambda i, j: (i, j)))
        args.append(residual)
        body = _mm_res_body
    return pl.pallas_call(
        body,
        grid=(m // tm, n // tn),
        in_specs=in_specs,
        out_specs=pl.BlockSpec((tm, tn), lambda i, j: (i, j)),
        out_shape=jax.ShapeDtypeStruct((m, n), F32),
        compiler_params=_cparams(("parallel", "arbitrary")),
        name=name,
    )(*args)


def _rope_tables(pos):
    half = ROPE_DIM // 2
    inv_freq = ROPE_THETA ** (-jnp.arange(half, dtype=F32) / half)
    ang = pos.astype(F32)[:, None] * inv_freq[None, :]
    cos, sin = jnp.cos(ang), jnp.sin(ang)
    n = pos.shape[0]
    ones = jnp.ones((n, HEAD_DIM - ROPE_DIM), F32)
    zeros = jnp.zeros((n, HEAD_DIM - ROPE_DIM), F32)
    c64 = jnp.concatenate([cos, cos, ones], axis=1)
    s64 = jnp.concatenate([-sin, sin, zeros], axis=1)
    return jnp.concatenate([c64, c64], axis=1), jnp.concatenate([s64, s64], axis=1)


def _rope(x, c, s):
    lane = lax.broadcasted_iota(jnp.int32, x.shape, 1) % HEAD_DIM
    partner = jnp.where(lane < ROPE_DIM // 2,
                        pltpu.roll(x, LANES - ROPE_DIM // 2, 1),
                        pltpu.roll(x, ROPE_DIM // 2, 1))
    return x * c + partner * s


def _dup_half(x, g):
    lane = lax.broadcasted_iota(jnp.int32, x.shape, 1)
    sw = pltpu.roll(x, HEAD_DIM, 1)
    if g == 0:
        return jnp.where(lane < HEAD_DIM, x, sw)
    return jnp.where(lane < HEAD_DIM, sw, x)


def _attn_frame(q_rows, cq, sq, kcat, vcat, valid, sink_ref, write):
    r = q_rows.shape[0]
    lane = lax.broadcasted_iota(jnp.int32, (r, LANES), 1)
    lane_v = lax.broadcasted_iota(jnp.int32, (3 * WINDOW, LANES), 1)
    for p in range(N_HEADS // 2):
        g = (2 * p) // (N_HEADS // N_KV)
        qp = _rope(q_rows[:, LANES * p:LANES * (p + 1)], cq, sq)
        acc = jnp.zeros((r, LANES), F32)
        for hh in range(2):
            h = 2 * p + hh
            in_half = (lane >= HEAD_DIM) if hh else (lane < HEAD_DIM)
            qm = jnp.where(in_half, qp, 0.0)
            logits = _bdot_nt(qm, kcat[g]) * (HEAD_DIM ** -0.5)
            logits = jnp.where(valid, logits, NEG_INF)
            sink = sink_ref[h]
            m = jnp.maximum(jnp.max(logits, axis=-1, keepdims=True), sink)
            pr = jnp.exp(logits - m)
            denom = jnp.sum(pr, axis=-1, keepdims=True) + jnp.exp(sink - m)
            v_half = jnp.where((lane_v >= HEAD_DIM) if hh else (lane_v < HEAD_DIM), vcat[g], 0.0)
            acc = acc + _bdot(pr / denom, v_half)
        write(p, acc)


def _frame_masks(r, no_prev):
    qi = lax.broadcasted_iota(jnp.int32, (r, 3 * WINDOW), 0)
    c = lax.broadcasted_iota(jnp.int32, (r, 3 * WINDOW), 1)
    prev_ok = (c < WINDOW) & (c > qi + no_prev * WINDOW)
    cur_ok = (c >= WINDOW) & (c < 2 * WINDOW) & (c - WINDOW <= qi)
    meta_ok = (c >= 2 * WINDOW) & (c < 2 * WINDOW + N_META)
    return prev_ok | cur_ok | meta_ok


def _attn_prompt_body(sink_ref, q_ref, k_ref, v_ref, c_ref, s_ref, o_ref, kr_ref):
    seq = q_ref.shape[0]
    nb = (seq - N_META) // WINDOW
    kr_ref[...] = _rope(k_ref[...], c_ref[...], s_ref[...])
    zpad = jnp.zeros((WINDOW - N_META, LANES), F32)
    kmeta = jnp.concatenate([kr_ref[0:N_META, :], zpad], axis=0)
    vmeta = jnp.concatenate([v_ref[0:N_META, :], zpad], axis=0)
    kmeta2 = [_dup_half(kmeta, g) for g in range(N_KV)]
    vmeta2 = [_dup_half(vmeta, g) for g in range(N_KV)]

    qi = lax.broadcasted_iota(jnp.int32, (N_META, 3 * WINDOW), 0)
    c = lax.broadcasted_iota(jnp.int32, (N_META, 3 * WINDOW), 1)
    valid_meta = (c >= 2 * WINDOW) & (c - 2 * WINDOW <= qi)
    kcat = [jnp.concatenate([kmeta2[g], kmeta2[g], kmeta2[g]], axis=0) for g in range(N_KV)]
    vcat = [jnp.concatenate([vmeta2[g], vmeta2[g], vmeta2[g]], axis=0) for g in range(N_KV)]

    def write_meta(p, val):
        o_ref[0:N_META, LANES * p:LANES * (p + 1)] = val

    _attn_frame(q_ref[0:N_META, :], c_ref[0:N_META, :], s_ref[0:N_META, :], kcat, vcat, valid_meta,
                sink_ref, write_meta)

    def frame(n, carry):
        r0 = pl.multiple_of(N_META + WINDOW * n, SUBLANES)
        p0 = pl.multiple_of(jnp.maximum(r0 - WINDOW, 0), SUBLANES)
        kprev, kcur = kr_ref[pl.ds(p0, WINDOW), :], kr_ref[pl.ds(r0, WINDOW), :]
        vprev, vcur = v_ref[pl.ds(p0, WINDOW), :], v_ref[pl.ds(r0, WINDOW), :]
        kc = [jnp.concatenate([_dup_half(kprev, g), _dup_half(kcur, g), kmeta2[g]], axis=0) for g in range(N_KV)]
        vc = [jnp.concatenate([_dup_half(vprev, g), _dup_half(vcur, g), vmeta2[g]], axis=0) for g in range(N_KV)]
        valid = _frame_masks(WINDOW, jnp.where(n == 0, 1, 0))

        def write(p, val):
            o_ref[pl.ds(r0, WINDOW), LANES * p:LANES * (p + 1)] = val

        _attn_frame(q_ref[pl.ds(r0, WINDOW), :], c_ref[pl.ds(r0, WINDOW), :], s_ref[pl.ds(r0, WINDOW), :],
                    kc, vc, valid, sink_ref, write)
        return carry

    lax.fori_loop(0, nb, frame, 0)


def attn_prompt(z, sinks, cos_t, sin_t, n_batch, seq):
    grid_spec = pltpu.PrefetchScalarGridSpec(
        num_scalar_prefetch=0,
        grid=(n_batch,),
        in_specs=[pl.BlockSpec(memory_space=pltpu.SMEM),
                  pl.BlockSpec((seq, D_MIX), lambda b: (b, Z_Q // D_MIX)),
                  pl.BlockSpec((seq, LANES), lambda b: (b, Z_AK // LANES)),
                  pl.BlockSpec((seq, LANES), lambda b: (b, Z_AV // LANES)),
                  pl.BlockSpec((seq, LANES), lambda b: (0, 0)),
                  pl.BlockSpec((seq, LANES), lambda b: (0, 0))],
        out_specs=[pl.BlockSpec((seq, D_MIX), lambda b: (b, 0)),
                   pl.BlockSpec((seq, LANES), lambda b: (b, 0))],
    )
    return pl.pallas_call(
        _attn_prompt_body,
        grid_spec=grid_spec,
        out_shape=[jax.ShapeDtypeStruct((n_batch * seq, D_MIX), F32),
                   jax.ShapeDtypeStruct((n_batch * seq, LANES), F32)],
        compiler_params=_cparams(("parallel",)),
        name="attn_prompt",
    )(sinks, z, z, z, cos_t, sin_t)


def _attn_sample_body(sink_ref, q_ref, k_ref, v_ref, c_ref, s_ref, mk_ref, mv_ref, wk_ref, wv_ref, o_ref, kr_ref,
                      *, n_seq, t_len):
    kr_ref[...] = _rope(k_ref[...], jnp.concatenate([c_ref[...]] * n_seq, axis=0),
                        jnp.concatenate([s_ref[...]] * n_seq, axis=0))
    zpad_m = jnp.zeros((WINDOW - N_META, LANES), F32)
    zpad_c = jnp.zeros((WINDOW - t_len, LANES), F32)
    valid = _frame_masks(t_len, 0)
    for s in range(n_seq):
        rows = slice(s * t_len, (s + 1) * t_len)
        kmeta = jnp.concatenate([mk_ref[s], zpad_m], axis=0)
        vmeta = jnp.concatenate([mv_ref[s], zpad_m], axis=0)
        kcur = jnp.concatenate([kr_ref[rows, :], zpad_c], axis=0)
        vcur = jnp.concatenate([v_ref[rows, :], zpad_c], axis=0)
        kprev, vprev = wk_ref[s], wv_ref[s]
        kc = [jnp.concatenate([_dup_half(kprev, g), _dup_half(kcur, g), _dup_half(kmeta, g)], axis=0)
              for g in range(N_KV)]
        vc = [jnp.concatenate([_dup_half(vprev, g), _dup_half(vcur, g), _dup_half(vmeta, g)], axis=0)
              for g in range(N_KV)]

        def write(p, val, rows=rows):
            o_ref[rows, LANES * p:LANES * (p + 1)] = val

        _attn_frame(q_ref[rows, :], c_ref[...], s_ref[...], kc, vc, valid, sink_ref, write)


def attn_sample(z, row0, sinks, cos_t, sin_t, meta_k, meta_v, win_k, win_v, n_seq_total, t_len, n_seq=8):
    rows = n_seq * t_len
    blk0 = row0 // rows
    grid_spec = pltpu.PrefetchScalarGridSpec(
        num_scalar_prefetch=0,
        grid=(n_seq_total // n_seq,),
        in_specs=[pl.BlockSpec(memory_space=pltpu.SMEM),
                  pl.BlockSpec((rows, D_MIX), lambda i: (blk0 + i, Z_Q // D_MIX)),
                  pl.BlockSpec((rows, LANES), lambda i: (blk0 + i, Z_AK // LANES)),
                  pl.BlockSpec((rows, LANES), lambda i: (blk0 + i, Z_AV // LANES)),
                  pl.BlockSpec((t_len, LANES), lambda i: (0, 0)),
                  pl.BlockSpec((t_len, LANES), lambda i: (0, 0)),
                  pl.BlockSpec((n_seq, N_META, LANES), lambda i: (i, 0, 0)),
                  pl.BlockSpec((n_seq, N_META, LANES), lambda i: (i, 0, 0)),
                  pl.BlockSpec((n_seq, WINDOW, LANES), lambda i: (i, 0, 0)),
                  pl.BlockSpec((n_seq, WINDOW, LANES), lambda i: (i, 0, 0))],
        out_specs=[pl.BlockSpec((rows, D_MIX), lambda i: (i, 0)),
                   pl.BlockSpec((rows, LANES), lambda i: (i, 0))],
    )
    return pl.pallas_call(
        functools.partial(_attn_sample_body, n_seq=n_seq, t_len=t_len),
        grid_spec=grid_spec,
        out_shape=[jax.ShapeDtypeStruct((n_seq_total * t_len, D_MIX), F32),
                   jax.ShapeDtypeStruct((n_seq_total * t_len, LANES), F32)],
        compiler_params=_cparams(("parallel",)),
        name="attn_sample",
    )(sinks, z, z, z, cos_t, sin_t, meta_k, meta_v, win_k, win_v)


def _gla_body(q_ref, k_ref, v_ref, r_ref, lr_ref, w2_ref, b_ref, norm_ref, h0_ref, o_ref, h_ref, ht_ref):
    seq = q_ref.shape[0]
    cs = GLA_CHUNK
    ht_ref[...] = h0_ref[...].T
    row = lax.broadcasted_iota(jnp.int32, (cs, cs), 0)
    col = lax.broadcasted_iota(jnp.int32, (cs, cs), 1)
    tril = row >= col
    tril_bf = jnp.where(tril, 1.0, 0.0).astype(BF16)

    def chunk(r0, nvalid):
        nload = min(cs, seq)

        def load(ref):
            x = ref[pl.ds(r0, nload), :]
            if nload < cs:
                x = jnp.concatenate([x, jnp.zeros((cs - nload, x.shape[1]), F32)], axis=0)
            return x

        q, k, v, gr, lr = load(q_ref), load(k_ref), load(v_ref), load(r_ref), load(lr_ref)
        pre = _bdot(lr, w2_ref[...]) + b_ref[...]
        la = (jnp.minimum(pre, 0.0) - jnp.log(1.0 + jnp.exp(-jnp.abs(pre)))) / GLA_NORMALIZER
        if nvalid < cs:
            live = lax.broadcasted_iota(jnp.int32, (cs, 1), 0) < nvalid
            la = jnp.where(live, la, 0.0)
            k = jnp.where(live, k, 0.0)
            v = jnp.where(live, v, 0.0)
        la_hi, la_lo = _split2(la)
        bc = (jnp.dot(tril_bf, la_hi, preferred_element_type=F32)
              + jnp.dot(tril_bf, la_lo, preferred_element_type=F32))
        b_last = bc[cs - 1:cs, :]
        q_in = q * jnp.exp(bc) * (GLA_DK ** -0.5)
        k_in = k * jnp.exp(-bc)
        k_out = k * jnp.exp(b_last - bc)
        att = jnp.where(tril, _bdot_nt(q_in, k_in), 0.0)
        ht = ht_ref[...]
        o = _bdot(att, v) + _bdot_nt(q_in, ht)
        ht_ref[...] = ht * jnp.exp(b_last) + _bdot_tn(v, k_out)
        o = o * lax.rsqrt(jnp.mean(o * o, axis=-1, keepdims=True) + EPS) * norm_ref[...]
        o = o * _silu(gr)
        o_ref[pl.ds(r0, nvalid), :] = o[0:nvalid, :]

    if seq < cs:
        chunk(0, seq)
    else:
        lead = seq % cs
        if lead:
            chunk(0, lead)

        def step(c, carry):
            chunk(pl.multiple_of(lead + cs * c, SUBLANES), cs)
            return carry

        lax.fori_loop(0, seq // cs, step, 0)
    h_ref[...] = ht_ref[...].T


def gla(z1, z2, blk0, n_seq, seq, w2p, b3, norm3, layer, h0):
    def zspec(width, col0):
        return pl.BlockSpec((seq, width), lambda s, h: (blk0 + s, col0 // width + h))

    return pl.pallas_call(
        _gla_body,
        grid=(n_seq, GLA_HEADS),
        in_specs=[zspec(GLA_DK, Z_GQ), zspec(GLA_DK, Z_GK), zspec(GLA_DV, Z_GV), zspec(GLA_DV, Z_GR),
                  pl.BlockSpec((seq, LANES), lambda s, h: (blk0 + s, Z_GLR // LANES)),
                  pl.BlockSpec((None, LANES, GLA_DK), lambda s, h: (layer, 0, h)),
                  pl.BlockSpec((None, 1, GLA_DK), lambda s, h: (layer, 0, h)),
                  pl.BlockSpec((None, 1, GLA_DV), lambda s, h: (layer, 0, 0)),
                  pl.BlockSpec((None, None, GLA_DK, GLA_DV), lambda s, h: (s, h, 0, 0))],
        out_specs=[pl.BlockSpec((seq, GLA_DV), lambda s, h: (s, h)),
                   pl.BlockSpec((None, None, GLA_DK, GLA_DV), lambda s, h: (s, h, 0, 0))],
        out_shape=[jax.ShapeDtypeStruct((n_seq * seq, D_MIX), F32),
                   jax.ShapeDtypeStruct((n_seq, GLA_HEADS, GLA_DK, GLA_DV), F32)],
        scratch_shapes=[pltpu.VMEM((GLA_DV, GLA_DK), F32)],
        compiler_params=_cparams(("parallel", "parallel")),
        name="gla",
    )(z1, z1, z1, z1, z2, w2p, b3, norm3, h0)


def _seg_ones(n, seg):
    r = lax.broadcasted_iota(jnp.int32, (n, n), 0) // seg
    c = lax.broadcasted_iota(jnp.int32, (n, n), 1) // seg
    return jnp.where(r == c, 1.0, 0.0).astype(BF16)


def _rw_pre_body(zr_ref, zk_ref, zv_ref, zl_ref, pr_ref, pk_ref, pv_ref, pl_ref,
                 mur_ref, muk_ref, muv_ref, mul_ref, w0_ref, w2_ref, a0_ref, a2_ref, g2_ref,
                 kk_ref, ka_ref, rk_ref,
                 r_out, w_out, k_out, v_out, kk_out, b_out, g_out, bonus_out):
    rows = zr_ref.shape[0]
    first = lax.broadcasted_iota(jnp.int32, (rows, 1), 0) == 0

    def shifted(z_ref, p_ref, mu_ref):
        z = z_ref[...]
        prev = jnp.where(first, p_ref[...], pltpu.roll(z, 1, 0))
        return z + (prev - z) * mu_ref[...]

    r = shifted(zr_ref, pr_ref, mur_ref)
    k = shifted(zk_ref, pk_ref, muk_ref)
    v = shifted(zv_ref, pv_ref, muv_ref)
    lo = shifted(zl_ref, pl_ref, mul_ref)
    zw, za, zg = lo[:, 0:LANES], lo[:, LANES:2 * LANES], lo[:, 2 * LANES:4 * LANES]
    w = -_softplus(-(w0_ref[...] + _bdot(jnp.tanh(zw), w2_ref[...]))) - 0.5
    decay = jnp.exp(-jnp.exp(w))
    a = _sigmoid(a0_ref[...] + _bdot(za, a2_ref[...]))
    g = _bdot(_sigmoid(zg), g2_ref[...])
    ones = _seg_ones(D_MIX, RW_HEAD)
    kkr = k * kk_ref[...]
    nrm = jnp.sqrt(_dot2(kkr * kkr, ones))
    kk = kkr / jnp.maximum(nrm, 1e-12)
    k2 = k * (1.0 + (a - 1.0) * ka_ref[...])
    r_out[...] = r
    w_out[...] = decay
    k_out[...] = k2
    v_out[...] = v
    kk_out[...] = kk
    b_out[...] = kk * a
    g_out[...] = g
    bonus_out[...] = _dot2(r * k2 * rk_ref[...], ones) * v


def rw_pre(z, prev, n_seq, tiles_per_seq, tt, blk0, lp, layer):
    nt = tiles_per_seq

    def zspec(width, col0):
        return pl.BlockSpec((tt, width), lambda s, j: (blk0 + s * nt + j, col0 // width))

    def pspec(width):
        return pl.BlockSpec((None, 1, width), lambda s, j: (s * nt + j, 0, 0))

    def lspec(width):
        return pl.BlockSpec((None, 1, width), lambda s, j: (layer, 0, 0))

    def wspec(rows_, cols_):
        return pl.BlockSpec((None, rows_, cols_), lambda s, j: (layer, 0, 0))

    tm_spec = pl.BlockSpec((tt, D_MIX), lambda s, j: (j, s))
    rm_spec = pl.BlockSpec((tt, D_MIX), lambda s, j: (s * nt + j, 0))
    tm_shape = jax.ShapeDtypeStruct((nt * tt, n_seq * D_MIX), F32)
    rm_shape = jax.ShapeDtypeStruct((n_seq * nt * tt, D_MIX), F32)
    return pl.pallas_call(
        _rw_pre_body,
        grid=(n_seq, nt),
        in_specs=[zspec(D_MIX, Z_RR), zspec(D_MIX, Z_RK), zspec(D_MIX, Z_RV), zspec(LORA_W, Z_LORA),
                  pspec(D_MIX), pspec(D_MIX), pspec(D_MIX), pspec(LORA_W),
                  lspec(D_MIX), lspec(D_MIX), lspec(D_MIX), lspec(LORA_W),
                  lspec(D_MIX), wspec(LANES, D_MIX), lspec(D_MIX), wspec(LANES, D_MIX), wspec(2 * LANES, D_MIX),
                  lspec(D_MIX), lspec(D_MIX), lspec(D_MIX)],
        out_specs=[tm_spec] * 6 + [rm_spec] * 2,
        out_shape=[tm_shape] * 6 + [rm_shape] * 2,
        compiler_params=_cparams(("parallel", "parallel")),
        name="rw_pre",
    )(z, z, z, z, prev["r"], prev["k"], prev["v"], prev["l"],
      lp["mu_r"], lp["mu_k"], lp["mu_v"], lp["mu_l"], lp["w0"], lp["w2"], lp["a0"], lp["a2"], lp["g2"],
      lp["kk"], lp["ka"], lp["rk"])


RW_GB = 4
RW_TILE = 256


def _rw_scan_body(r_ref, w_ref, k_ref, v_ref, kk_ref, b_ref, s0_ref, y_ref, s_out_ref, s_ref):
    tb = r_ref.shape[0]
    nj = D_MIX // RW_TILE
    hpt = RW_TILE // RW_HEAD

    @pl.when(pl.program_id(1) == 0)
    def _():
        s_ref[...] = s0_ref[...]

    ones4 = _seg_ones(RW_TILE, RW_HEAD)
    ones_stack = jnp.concatenate([ones4, ones4], axis=0)
    rr = lax.broadcasted_iota(jnp.int32, (RW_HEAD, RW_TILE), 0)
    cc = lax.broadcasted_iota(jnp.int32, (RW_HEAD, RW_TILE), 1)
    eye_rep = jnp.where(cc % RW_HEAD == rr, 1.0, 0.0).astype(BF16)
    hr = lax.broadcasted_iota(jnp.int32, (SUBLANES, RW_TILE), 0)
    hc = lax.broadcasted_iota(jnp.int32, (SUBLANES, RW_TILE), 1)
    head_rows = jnp.where(hc // RW_HEAD == hr, 1.0, 0.0)
    tiles = [(bi, j) for bi in range(RW_GB) for j in range(nj)]

    def step(t, carry):
        def row(ref, bi, j):
            return ref[pl.ds(t, 1), D_MIX * bi + RW_TILE * j:D_MIX * bi + RW_TILE * (j + 1)]

        p_list, vd_list, s_old = [], [], []
        for bi, j in tiles:
            s = s_ref[bi, :, RW_TILE * j:RW_TILE * (j + 1)]
            s_old.append(s)
            p_list.append((s * row(kk_ref, bi, j)).astype(BF16))
            v_hi, v_lo = _split2(row(v_ref, bi, j))
            vd_list.append(jnp.concatenate([eye_rep * v_hi, eye_rep * v_lo], axis=1))
        sa_all = jnp.dot(jnp.concatenate(p_list, axis=0), ones4, preferred_element_type=F32)
        vcol_all = jnp.dot(jnp.concatenate(vd_list, axis=0), ones_stack, preferred_element_type=F32)
        for idx, (bi, j) in enumerate(tiles):
            lanes = slice(RW_TILE * j, RW_TILE * (j + 1))
            rows = slice(RW_HEAD * idx, RW_HEAD * (idx + 1))
            s_new = (s_old[idx] * row(w_ref, bi, j) - sa_all[rows] * row(b_ref, bi, j)
                     + vcol_all[rows] * row(k_ref, bi, j))
            s_ref[bi, :, lanes] = s_new
            y8 = lax.dot_general((head_rows * row(r_ref, bi, j)).astype(BF16), s_new.astype(BF16),
                                 (((1,), (1,)), ((), ())), preferred_element_type=F32)
            y_ref[t, RW_HEADS * bi + hpt * j:RW_HEADS * bi + hpt * (j + 1), :] = y8[0:hpt, :]
        return carry

    lax.fori_loop(0, tb, step, 0, unroll=2)

    @pl.when(pl.program_id(1) == pl.num_programs(1) - 1)
    def _():
        s_out_ref[...] = s_ref[...]


def rw_scan(seqs, s0, tb):
    steps, width = seqs[0].shape
    n_seq = width // D_MIX
    gw = RW_GB * D_MIX
    in_spec = pl.BlockSpec((tb, gw), lambda g, i: (i, g))
    st_spec = pl.BlockSpec((RW_GB, RW_HEAD, D_MIX), lambda g, i: (g, 0, 0))
    return pl.pallas_call(
        _rw_scan_body,
        grid=(n_seq // RW_GB, steps // tb),
        in_specs=[in_spec] * 6 + [st_spec],
        out_specs=[pl.BlockSpec((tb, RW_GB * RW_HEADS, RW_HEAD), lambda g, i: (i, g, 0)), st_spec],
        out_shape=[jax.ShapeDtypeStruct((steps, n_seq * RW_HEADS, RW_HEAD), F32),
                   jax.ShapeDtypeStruct((n_seq, RW_HEAD, D_MIX), F32)],
        scratch_shapes=[pltpu.VMEM((RW_GB, RW_HEAD, D_MIX), F32)],
        compiler_params=_cparams(("parallel", "arbitrary")),
        name="rw_scan",
    )(*seqs, s0)


def _rw_post_body(y_ref, g_ref, bonus_ref, lng_ref, lnb_ref, o_ref):
    y = y_ref[...]
    ones = _seg_ones(D_MIX, RW_HEAD)
    mu = _dot2(y, ones) * (1.0 / RW_HEAD)
    d = y - mu
    var = _dot2(d * d, ones) * (1.0 / RW_HEAD)
    yn = d * lax.rsqrt(var + RW_LN_EPS) * lng_ref[...] + lnb_ref[...]
    o_ref[...] = (yn + bonus_ref[...]) * g_ref[...]


def rw_post(y_tm, g, bonus, n_seq, tiles_per_seq, tt, lng3, lnb3, layer):
    nt = tiles_per_seq
    rm_spec = pl.BlockSpec((tt, D_MIX), lambda s, j: (s * nt + j, 0))
    lspec = pl.BlockSpec((None, 1, D_MIX), lambda s, j: (layer, 0, 0))
    return pl.pallas_call(
        _rw_post_body,
        grid=(n_seq, nt),
        in_specs=[pl.BlockSpec((tt, D_MIX), lambda s, j: (j, s)), rm_spec, rm_spec, lspec, lspec],
        out_specs=rm_spec,
        out_shape=jax.ShapeDtypeStruct((n_seq * nt * tt, D_MIX), F32),
        compiler_params=_cparams(("parallel", "parallel")),
        name="rw_post",
    )(y_tm, g, bonus, lng3, lnb3)


S5_W = S5_GROUPS * S5_STATE
S5_KT = 256
S5_NT = S5_KT // S5_GROUP * S5_STATE


def _s5_body(u_ref, bbr_ref, bbi_ref, ccr_ref, cci_ref, d_ref, a_ref, apow_ref, gw_ref, gb_ref, h0r_ref, h0i_ref,
             o_ref, hr_out, hi_out, xr_ref, xi_ref, cr_ref, ci_ref):
    tt = u_ref.shape[0]
    nkt = D_MIX // S5_KT

    @pl.when(pl.program_id(1) == 0)
    def _():
        cr_ref[...] = h0r_ref[...]
        ci_ref[...] = h0i_ref[...]

    u = u_ref[...]
    for jt in range(nkt):
        ub = u[:, S5_KT * jt:S5_KT * (jt + 1)]
        xr_ref[:, S5_NT * jt:S5_NT * (jt + 1)] = _bdot(ub, bbr_ref[jt])
        xi_ref[:, S5_NT * jt:S5_NT * (jt + 1)] = _bdot(ub, bbi_ref[jt])

    sub = lax.broadcasted_iota(jnp.int32, (SUBLANES, S5_W), 0)
    a_re = [a_ref[2 * i:2 * i + 1, :] for i in range(3)]
    a_im = [a_ref[2 * i + 1:2 * i + 2, :] for i in range(3)]
    pw_re, pw_im = apow_ref[0], apow_ref[1]

    def group(gi, carry):
        r0 = pl.multiple_of(gi * SUBLANES, SUBLANES)
        hr, hi = xr_ref[pl.ds(r0, SUBLANES), :], xi_ref[pl.ds(r0, SUBLANES), :]
        for i, sh in enumerate((1, 2, 4)):
            keep = sub >= sh
            sr = jnp.where(keep, pltpu.roll(hr, sh, 0), 0.0)
            si = jnp.where(keep, pltpu.roll(hi, sh, 0), 0.0)
            hr, hi = hr + a_re[i] * sr - a_im[i] * si, hi + a_re[i] * si + a_im[i] * sr
        c_r, c_i = cr_ref[...], ci_ref[...]
        hr = hr + pw_re * c_r - pw_im * c_i
        hi = hi + pw_re * c_i + pw_im * c_r
        xr_ref[pl.ds(r0, SUBLANES), :] = hr
        xi_ref[pl.ds(r0, SUBLANES), :] = hi
        cr_ref[...] = hr[SUBLANES - 1:SUBLANES, :]
        ci_ref[...] = hi[SUBLANES - 1:SUBLANES, :]
        return carry

    lax.fori_loop(0, tt // SUBLANES, group, 0)

    ys = []
    for jt in range(nkt):
        hr = xr_ref[:, S5_NT * jt:S5_NT * (jt + 1)]
        hi = xi_ref[:, S5_NT * jt:S5_NT * (jt + 1)]
        ys.append(_bdot(hr, ccr_ref[jt]) - _bdot(hi, cci_ref[jt]))
    y = jnp.concatenate(ys, axis=1) + d_ref[...] * u
    ya = 0.5 * y * (1.0 + jnp.tanh(math.sqrt(2.0 / math.pi) * (y + 0.044715 * (y * y * y))))
    o_ref[...] = ya * _sigmoid(_bdot(ya, gw_ref[...]) + gb_ref[...])

    @pl.when(pl.program_id(1) == pl.num_programs(1) - 1)
    def _():
        hr_out[...] = cr_ref[...]
        hi_out[...] = ci_ref[...]


def s5(z, blk0, n_seq, tiles_per_seq, tt, sp, layer, h0r, h0i):
    nt = tiles_per_seq
    nkt = D_MIX // S5_KT

    def const3(shape):
        return pl.BlockSpec(shape, lambda s, j: (0,) * len(shape))

    st_spec = pl.BlockSpec((None, 1, S5_W), lambda s, j: (s, 0, 0))
    return pl.pallas_call(
        _s5_body,
        grid=(n_seq, nt),
        in_specs=[pl.BlockSpec((tt, D_MIX), lambda s, j: (blk0 + s * nt + j, Z_S5 // D_MIX)),
                  const3((nkt, S5_KT, S5_NT)), const3((nkt, S5_KT, S5_NT)),
                  const3((nkt, S5_NT, S5_KT)), const3((nkt, S5_NT, S5_KT)),
                  const3((1, D_MIX)), const3((6, S5_W)), const3((2, SUBLANES, S5_W)),
                  pl.BlockSpec((None, D_MIX, D_MIX), lambda s, j: (layer, 0, 0)),
                  pl.BlockSpec((None, 1, D_MIX), lambda s, j: (layer, 0, 0)),
                  st_spec, st_spec],
        out_specs=[pl.BlockSpec((tt, D_MIX), lambda s, j: (s * nt + j, 0)), st_spec, st_spec],
        out_shape=[jax.ShapeDtypeStruct((n_seq * nt * tt, D_MIX), F32),
                   jax.ShapeDtypeStruct((n_seq, 1, S5_W), F32),
                   jax.ShapeDtypeStruct((n_seq, 1, S5_W), F32)],
        scratch_shapes=[pltpu.VMEM((tt, S5_W), F32), pltpu.VMEM((tt, S5_W), F32),
                        pltpu.VMEM((1, S5_W), F32), pltpu.VMEM((1, S5_W), F32)],
        compiler_params=_cparams(("parallel", "arbitrary")),
        name="s5",
    )(z, sp["bbr"], sp["bbi"], sp["ccr"], sp["cci"], sp["d"], sp["a"], sp["apow"], sp["glu_w"], sp["glu_b"], h0r, h0i)


def _s5_tables(lam_re, lam_im, log_step, b_re, b_im, c_re, c_im, d):
    dt = jnp.exp(log_step.astype(F32))[:, None]
    lr = jnp.minimum(lam_re.astype(F32), -1e-4)
    li = lam_im.astype(F32)
    mag = jnp.exp(lr * dt)
    ab_re, ab_im = mag * jnp.cos(li * dt), mag * jnp.sin(li * dt)
    den = lr * lr + li * li
    f_re = ((ab_re - 1.0) * lr + ab_im * li) / den
    f_im = (ab_im * lr - (ab_re - 1.0) * li) / den
    bb_re = f_re[..., None] * b_re - f_im[..., None] * b_im
    bb_im = f_re[..., None] * b_im + f_im[..., None] * b_re
    gpt = S5_KT // S5_GROUP
    nkt = S5_GROUPS // gpt
    eye = jnp.eye(gpt, dtype=F32)

    def in_tiles(bb):
        t = bb.reshape(nkt, gpt, S5_STATE, S5_GROUP)
        t = jnp.einsum('jgph,gk->jghkp', t, eye)
        return t.reshape(nkt, S5_KT, S5_NT)

    def out_tiles(cc):
        t = cc.reshape(nkt, gpt, S5_GROUP, S5_STATE)
        t = jnp.einsum('jghp,gk->jgpkh', t, eye)
        return t.reshape(nkt, S5_NT, S5_KT)

    ar, ai = ab_re.reshape(1, S5_W), ab_im.reshape(1, S5_W)
    pows_r, pows_i = [ar], [ai]
    for _ in range(SUBLANES - 1):
        pr, pi = pows_r[-1], pows_i[-1]
        pows_r.append(pr * ar - pi * ai)
        pows_i.append(pr * ai + pi * ar)
    a_tab = jnp.concatenate([pows_r[0], pows_i[0], pows_r[1], pows_i[1], pows_r[3], pows_i[3]], axis=0)
    apow = jnp.stack([jnp.concatenate(pows_r, axis=0), jnp.concatenate(pows_i, axis=0)])
    return {"bbr": in_tiles(bb_re).astype(BF16), "bbi": in_tiles(bb_im).astype(BF16),
            "ccr": out_tiles(c_re.astype(F32)).astype(BF16), "cci": out_tiles(c_im.astype(F32)).astype(BF16), "d": d.reshape(1, D_MIX).astype(F32), "a": a_tab, "apow": apow}


def _merge_body(o0, o1, o2, o3, wb_ref, g0, g1, g2, g3, out_ref):
    acc = None
    for i, (o_ref, g_ref) in enumerate(((o0, g0), (o1, g1), (o2, g2), (o3, g3))):
        term = _sigmoid(g_ref[...]) * _bdot(o_ref[...], wb_ref[i])
        acc = term if acc is None else acc + term
    out_ref[...] = acc.astype(out_ref.dtype)


def merge(outs, w_branch, zgate, layer, *, tm, tn):
    m = outs[0].shape[0]
    nj = D_MODEL // tn
    o_spec = pl.BlockSpec((tm, D_MIX), lambda i, j: (i, 0))

    def gspec(b):
        return pl.BlockSpec((tm, tn), lambda i, j: (i, b * nj + j))

    return pl.pallas_call(
        _merge_body,
        grid=(m // tm, nj),
        in_specs=[o_spec] * 4 + [pl.BlockSpec((None, N_BRANCH, D_MIX, tn), lambda i, j: (layer, 0, 0, j))]
        + [gspec(b) for b in range(N_BRANCH)],
        out_specs=pl.BlockSpec((tm, tn), lambda i, j: (i, j)),
        out_shape=jax.ShapeDtypeStruct((m, D_MODEL), BF16),
        compiler_params=_cparams(("parallel", "arbitrary")),
        name="merge",
    )(*outs, w_branch, zgate, zgate, zgate, zgate)


def _router_body(x_ref, g_ref, wr_ref, br_ref, xn_ref, route_ref):
    x = x_ref[...]
    xn = x * lax.rsqrt(jnp.mean(x * x, axis=-1, keepdims=True) + EPS) * g_ref[...]
    xn_ref[...] = xn
    logits = _bdot(xn, wr_ref[...]) + br_ref[...]
    lane = lax.broadcasted_iota(jnp.int32, logits.shape, 1)
    big = jnp.int32(1 << 20)
    ninf = -jnp.inf
    lg = jnp.where(lane < N_GROUPS, logits, ninf)
    gmax = jnp.max(lg, axis=-1, keepdims=True)
    g_top = jnp.min(jnp.where(lg == gmax, lane, big), axis=-1, keepdims=True)
    pg = 1.0 / jnp.sum(jnp.exp(lg - gmax), axis=-1, keepdims=True)
    e_lane = lane - N_GROUPS
    in_group = (e_lane >= 0) & (e_lane < N_EXPERTS) & ((e_lane // EXP_PER_GROUP) == g_top)
    le = jnp.where(in_group, logits, ninf)
    m1 = jnp.max(le, axis=-1, keepdims=True)
    i1 = jnp.min(jnp.where(le == m1, lane, big), axis=-1, keepdims=True)
    le2 = jnp.where(lane == i1, ninf, le)
    m2 = jnp.max(le2, axis=-1, keepdims=True)
    i2 = jnp.min(jnp.where(le2 == m2, lane, big), axis=-1, keepdims=True)
    e2 = jnp.exp(m2 - m1)
    w1 = pg / (1.0 + e2)
    w2 = pg * e2 / (1.0 + e2)
    out = jnp.where(lane == 0, (i1 - N_GROUPS).astype(F32),
                    jnp.where(lane == 1, (i2 - N_GROUPS).astype(F32),
                              jnp.where(lane == 2, w1, jnp.where(lane == 3, w2, 0.0))))
    route_ref[...] = out


def router(x, g3, wr3, br3, layer, tm):
    m, d = x.shape
    return pl.pallas_call(
        _router_body,
        grid=(m // tm,),
        in_specs=[pl.BlockSpec((tm, d), lambda i: (i, 0)),
                  pl.BlockSpec((None, 1, d), lambda i: (layer, 0, 0)),
                  pl.BlockSpec((None, d, LANES), lambda i: (layer, 0, 0)),
                  pl.BlockSpec((None, 1, LANES), lambda i: (layer, 0, 0))],
        out_specs=[pl.BlockSpec((tm, d), lambda i: (i, 0)), pl.BlockSpec((tm, LANES), lambda i: (i, 0))],
        out_shape=[jax.ShapeDtypeStruct((m, d), F32), jax.ShapeDtypeStruct((m, LANES), F32)],
        compiler_params=_cparams(("parallel",)),
        name="moe_router",
    )(x, g3, wr3, br3)


def _gather_rows(src_hbm, idx_ref, base, dst, sem, n):
    for r in range(n):
        pltpu.make_async_copy(src_hbm.at[pl.ds(idx_ref[base + r], 1), :], dst.at[pl.ds(r, 1), :], sem).start()


def _wait_rows(src_hbm, dst, sem, n):
    for r in range(n):
        pltpu.make_async_copy(src_hbm.at[pl.ds(0, 1), :], dst.at[pl.ds(r, 1), :], sem).wait()


def _moe_up_body(exp_ref, tok_ref, x_hbm, w1_ref, w3_ref, h_ref, xbuf, sems):
    i = pl.program_id(0)
    nblk = pl.num_programs(0)
    slot = i % 2

    @pl.when(i == 0)
    def _():
        _gather_rows(x_hbm, tok_ref, 0, xbuf.at[0], sems.at[0], MOE_BLOCK)

    @pl.when(i + 1 < nblk)
    def _():
        _gather_rows(x_hbm, tok_ref, (i + 1) * MOE_BLOCK, xbuf.at[1 - slot], sems.at[1 - slot], MOE_BLOCK)

    _wait_rows(x_hbm, xbuf.at[slot], sems.at[slot], MOE_BLOCK)
    xb = xbuf[slot].astype(BF16)
    a = jnp.dot(xb, w1_ref[...].astype(BF16), preferred_element_type=F32)
    b = jnp.dot(xb, w3_ref[...].astype(BF16), preferred_element_type=F32)
    h_ref[...] = _silu(a) * b


def moe_up(blk_exp, slot_tok, xn, w1, w3, layer):
    n_blocks = blk_exp.shape[0]
    d = xn.shape[1]
    grid_spec = pltpu.PrefetchScalarGridSpec(
        num_scalar_prefetch=2,
        grid=(n_blocks,),
        in_specs=[pl.BlockSpec(memory_space=pl.ANY),
                  pl.BlockSpec((None, None, d, D_EXPERT), lambda i, e, t: (layer, e[i], 0, 0)),
                  pl.BlockSpec((None, None, d, D_EXPERT), lambda i, e, t: (layer, e[i], 0, 0))],
        out_specs=pl.BlockSpec((MOE_BLOCK, D_EXPERT), lambda i, e, t: (i, 0)),
        scratch_shapes=[pltpu.VMEM((2, MOE_BLOCK, d), F32), pltpu.SemaphoreType.DMA((2,))],
    )
    return pl.pallas_call(
        _moe_up_body,
        grid_spec=grid_spec,
        out_shape=jax.ShapeDtypeStruct((n_blocks * MOE_BLOCK, D_EXPERT), F32),
        compiler_params=_cparams(("arbitrary",)),
        name="moe_up",
    )(blk_exp, slot_tok, xn, w1, w3)


def _moe_down_body(exp_ref, h_ref, w2_ref, sw_ref, y_ref):
    y_ref[...] = jnp.dot(h_ref[...].astype(BF16), w2_ref[...].astype(BF16), preferred_element_type=F32) * sw_ref[...]


def moe_down(blk_exp, h, w2, slot_w, layer):
    n_blocks = blk_exp.shape[0]
    d = w2.shape[3]
    grid_spec = pltpu.PrefetchScalarGridSpec(
        num_scalar_prefetch=1,
        grid=(n_blocks,),
        in_specs=[pl.BlockSpec((MOE_BLOCK, D_EXPERT), lambda i, e: (i, 0)),
                  pl.BlockSpec((None, None, D_EXPERT, d), lambda i, e: (layer, e[i], 0, 0)),
                  pl.BlockSpec((MOE_BLOCK, 1), lambda i, e: (i, 0))],
        out_specs=pl.BlockSpec((MOE_BLOCK, d), lambda i, e: (i, 0)),
    )
    return pl.pallas_call(
        _moe_down_body,
        grid_spec=grid_spec,
        out_shape=jax.ShapeDtypeStruct((n_blocks * MOE_BLOCK, d), F32),
        compiler_params=_cparams(("arbitrary",)),
        name="moe_down",
    )(blk_exp, h, w2, slot_w)


MOE_TC = 64


def _moe_combine_body(slot_ref, x_ref, y_hbm, o_ref, ybuf, sems):
    i = pl.program_id(0)
    nstep = pl.num_programs(0)
    slot = i % 2
    n = TOP_K * MOE_TC

    @pl.when(i == 0)
    def _():
        _gather_rows(y_hbm, slot_ref, 0, ybuf.at[0], sems.at[0], n)

    @pl.when(i + 1 < nstep)
    def _():
        _gather_rows(y_hbm, slot_ref, (i + 1) * n, ybuf.at[1 - slot], sems.at[1 - slot], n)

    _wait_rows(y_hbm, ybuf.at[slot], sems.at[slot], n)
    o_ref[...] = x_ref[...] + (ybuf[slot, 0:MOE_TC, :] + ybuf[slot, MOE_TC:n, :])


def moe_combine(slot_of, x, yb):
    m, d = x.shape
    grid_spec = pltpu.PrefetchScalarGridSpec(
        num_scalar_prefetch=1,
        grid=(m // MOE_TC,),
        in_specs=[pl.BlockSpec((MOE_TC, d), lambda i, s: (i, 0)),
                  pl.BlockSpec(memory_space=pl.ANY)],
        out_specs=pl.BlockSpec((MOE_TC, d), lambda i, s: (i, 0)),
        scratch_shapes=[pltpu.VMEM((2, TOP_K * MOE_TC, d), F32), pltpu.SemaphoreType.DMA((2,))],
    )
    return pl.pallas_call(
        _moe_combine_body,
        grid_spec=grid_spec,
        out_shape=jax.ShapeDtypeStruct((m, d), F32),
        compiler_params=_cparams(("arbitrary",)),
        name="moe_combine",
    )(slot_of, x, yb)


def _moe_plan(route, n_tok):
    e_idx = route[:, 0:TOP_K].astype(jnp.int32)
    w = route[:, TOP_K:2 * TOP_K]
    n_assign = n_tok * TOP_K
    e_flat = e_idx.reshape(n_assign)
    onehot = (e_flat[:, None] == jnp.arange(N_EXPERTS, dtype=jnp.int32)[None, :]).astype(jnp.int32)
    csum = jnp.cumsum(onehot, axis=0)
    counts = csum[-1]
    rank = jnp.sum((csum - onehot) * onehot, axis=1)
    padded = (counts + MOE_BLOCK - 1) // MOE_BLOCK * MOE_BLOCK
    pad_end = jnp.cumsum(padded)
    pad_start = pad_end - padded
    dest = pad_start[e_flat] + rank
    n_blocks = -(-(n_assign + N_EXPERTS * (MOE_BLOCK - 1)) // MOE_BLOCK)
    n_slots = n_blocks * MOE_BLOCK
    tok_flat = jnp.repeat(jnp.arange(n_tok, dtype=jnp.int32), TOP_K)
    packed = jnp.stack([tok_flat.astype(F32), w.reshape(n_assign)], axis=1)
    slots = jnp.zeros((n_slots, 2), F32).at[dest].set(packed)
    slot_tok = slots[:, 0].astype(jnp.int32)
    slot_w = slots[:, 1:2]
    blk_start = jnp.arange(n_blocks, dtype=jnp.int32) * MOE_BLOCK
    blk_exp = jnp.minimum(jnp.sum((pad_end[None, :] <= blk_start[:, None]).astype(jnp.int32), axis=1),
                          N_EXPERTS - 1).astype(jnp.int32)
    d2 = dest.reshape(n_tok // MOE_TC, MOE_TC, TOP_K)
    slot_of = jnp.transpose(d2, (0, 2, 1)).reshape(-1)
    return blk_exp, slot_tok, slot_w, slot_of


def kernel(x_prompt, x_sample, cache_attn_meta_k, cache_attn_meta_v, cache_attn_win_k, cache_attn_win_v, state_gla, state_rwkv, state_rwkv_shift, state_s5_re, state_s5_im, meta_tokens, norm_mix, w_in, w_gla_gate, b_gla_gate, gla_norm, attn_sinks, rw_mu, rw_w0, rw_w2, rw_a0, rw_a2, rw_g2, rw_kk, rw_ka, rw_rk, rw_ln_g, rw_ln_b, s5_lam_re, s5_lam_im, s5_log_step, s5_b_re, s5_b_im, s5_c_re, s5_c_im, s5_d, s5_glu_w, s5_glu_b, w_branch, w_out, norm_moe, w_router_group, b_router_group, w_router_expert, b_router_expert, w_exp_gate, w_exp_up, w_exp_down, norm_final):
    depth = w_in.shape[0]
    nb, seq_p = x_prompt.shape[0], x_prompt.shape[1]
    db, t_len = x_sample.shape[0], x_sample.shape[1]
    seq = seq_p + N_META
    rows_p = nb * seq
    rows_s = db * t_len
    n_tok = rows_p + rows_s
    wb_len = cache_attn_win_k.shape[2]

    tm_big = n_tok // 7
    tm_mid = n_tok // 14
    tm_small = n_tok // 19
    rw_nt = 6
    rw_tt = seq // rw_nt
    rw_tb = seq // 43
    s5_tt = seq // 6

    meta = jnp.broadcast_to(meta_tokens.astype(F32)[None], (nb, N_META, D_MODEL))
    x = jnp.concatenate([jnp.concatenate([meta, x_prompt], axis=1).reshape(rows_p, D_MODEL),
                         x_sample.reshape(rows_s, D_MODEL)], axis=0)

    cos_p, sin_p = _rope_tables(jnp.arange(seq, dtype=jnp.int32))
    cos_s, sin_s = _rope_tables(PAST_LEN + jnp.arange(t_len, dtype=jnp.int32))

    def r3(a):
        return a.reshape(depth, 1, a.shape[-1]).astype(F32)

    norm_mix3, norm_moe3 = r3(norm_mix), r3(norm_moe)
    b_gla3, gla_norm3 = r3(b_gla_gate), r3(gla_norm)
    w2p = jnp.pad(w_gla_gate, ((0, 0), (0, LANES - GLA_RANK), (0, 0)))

    def lora_cols(a):
        c = 3 * D_MIX
        zw, za, zg = a[..., c:c + 64], a[..., c + 64:c + 128], a[..., c + 128:c + 288]
        pad = lambda t, n: jnp.pad(t, [(0, 0)] * (t.ndim - 1) + [(0, n - t.shape[-1])])
        lo = jnp.concatenate([pad(zw, LANES), pad(za, LANES), pad(zg, 2 * LANES)], axis=-1)
        return a[..., 0:D_MIX], a[..., D_MIX:2 * D_MIX], a[..., 2 * D_MIX:c], lo

    mu_r, mu_k, mu_v, mu_l = lora_cols(rw_mu)
    rw_params = {
        "mu_r": r3(mu_r), "mu_k": r3(mu_k), "mu_v": r3(mu_v), "mu_l": r3(mu_l),
        "w0": r3(rw_w0), "a0": r3(rw_a0), "kk": r3(rw_kk), "ka": r3(rw_ka),
        "rk": r3(rw_rk.reshape(depth, D_MIX)),
        "w2": jnp.pad(rw_w2, ((0, 0), (0, LANES - RW_DECAY_RANK), (0, 0))),
        "a2": jnp.pad(rw_a2, ((0, 0), (0, LANES - RW_A_RANK), (0, 0))),
        "g2": jnp.pad(rw_g2, ((0, 0), (0, 2 * LANES - RW_G_RANK), (0, 0))),
    }
    ln_g3, ln_b3 = r3(rw_ln_g), r3(rw_ln_b)
    glu_b3 = r3(s5_glu_b)
    wr3 = jnp.pad(jnp.concatenate([w_router_group, w_router_expert], axis=2).astype(F32),
                  ((0, 0), (0, 0), (0, LANES - N_GROUPS - N_EXPERTS)))
    br3 = jnp.pad(jnp.concatenate([b_router_group, b_router_expert], axis=1).astype(F32),
                  ((0, 0), (0, LANES - N_GROUPS - N_EXPERTS))).reshape(depth, 1, LANES)

    rec = [[] for _ in range(16)]
    zeros_gla = jnp.zeros((nb, GLA_HEADS, GLA_DK, GLA_DV), F32)
    zeros_rw = jnp.zeros((nb, RW_HEAD, D_MIX), F32)
    zeros_s5 = jnp.zeros((nb, 1, S5_W), F32)

    for l in range(depth):
        zcol = lambda n: jnp.zeros((D_MODEL, n), F32)
        wcols = lambda a, n: lax.slice(w_in, (l, 0, a), (l + 1, D_MODEL, a + n))[0]
        rw0 = _SRC_RW
        w_mix = jnp.concatenate([
            wcols(rw0, 3072), wcols(_SRC_S5, 1024),
            wcols(rw0 + 3072, 64), zcol(64), wcols(rw0 + 3136, 64), zcol(64), wcols(rw0 + 3200, 160), zcol(96),
            wcols(_SRC_GLR, 16), zcol(112), zcol(NZ_2 - Z_GLR - LANES)], axis=1)[None]
        w_gate = wcols(_SRC_GATE, N_BRANCH * D_MODEL)[None]
        xn = rmsnorm_rows(x, norm_mix3, l, BF16, tm_small)
        z1 = matmul_rows(xn, w_in, l, tm=tm_big, tn=256, n=NZ_1, name="in_proj_a")
        z2 = matmul_rows(xn, w_mix, 0, tm=tm_big, tn=512, name="in_proj_b")
        zgate = matmul_rows(xn, w_gate, 0, tm=tm_big, tn=512, name="in_proj_gate")

        sinks = attn_sinks[l].astype(F32)
        o_att_p, kr_p = attn_prompt(z1, sinks, cos_p, sin_p, nb, seq)
        mk = cache_attn_meta_k[l].reshape(db, N_META, LANES)
        mv = cache_attn_meta_v[l].reshape(db, N_META, LANES)
        wk = cache_attn_win_k[l].reshape(db, wb_len, LANES)
        wv = cache_attn_win_v[l].reshape(db, wb_len, LANES)
        o_att_s, kr_s = attn_sample(z1, rows_p, sinks, cos_s, sin_s, mk, mv, wk, wv, db, t_len)
        v_p = z1[:rows_p, Z_AV:Z_AV + LANES].reshape(nb, seq, N_KV, HEAD_DIM)
        k_p = kr_p.reshape(nb, seq, N_KV, HEAD_DIM)
        v_s = z1[rows_p:, Z_AV:Z_AV + LANES].reshape(db, t_len, N_KV, HEAD_DIM)
        k_s = kr_s.reshape(db, t_len, N_KV, HEAD_DIM)
        new_wk = jnp.concatenate([cache_attn_win_k[l].astype(F32), k_s], axis=1)[:, -wb_len:]
        new_wv = jnp.concatenate([cache_attn_win_v[l].astype(F32), v_s], axis=1)[:, -wb_len:]

        o_gla_p, gla_p = gla(z1, z2, 0, nb, seq, w2p, b_gla3, gla_norm3, l, zeros_gla)
        o_gla_s, gla_s = gla(z1, z2, rows_p // t_len, db, t_len, w2p, b_gla3, gla_norm3, l,
                             state_gla[l].astype(F32))

        def prev_rows(col0, width):
            rows_ = []
            for bq in range(nb):
                for jq in range(rw_nt):
                    r_ = bq * seq + jq * rw_tt
                    rows_.append(jnp.zeros((1, width), F32) if jq == 0 else z2[r_ - 1:r_, col0:col0 + width])
            return jnp.stack(rows_)

        prev_p = {"r": prev_rows(Z_RR, D_MIX), "k": prev_rows(Z_RK, D_MIX), "v": prev_rows(Z_RV, D_MIX),
                  "l": prev_rows(Z_LORA, LORA_W)}
        sr, sk, sv, sl = lora_cols(state_rwkv_shift[l].astype(F32))
        prev_s = {"r": sr[:, None], "k": sk[:, None], "v": sv[:, None], "l": sl[:, None]}

        pre_p = rw_pre(z2, prev_p, nb, rw_nt, rw_tt, 0, rw_params, l)
        pre_s = rw_pre(z2, prev_s, db, 1, t_len, rows_p // t_len, rw_params, l)
        y_p, st_p = rw_scan(pre_p[:6], zeros_rw, rw_tb)
        s0_s = jnp.transpose(state_rwkv[l].astype(F32), (0, 2, 1, 3)).reshape(db, RW_HEAD, D_MIX)
        y_s, st_s = rw_scan(pre_s[:6], s0_s, t_len)
        o_rw_p = rw_post(y_p.reshape(seq, nb * D_MIX), pre_p[6], pre_p[7], nb, rw_nt, rw_tt, ln_g3, ln_b3, l)
        o_rw_s = rw_post(y_s.reshape(t_len, db * D_MIX), pre_s[6], pre_s[7], db, 1, t_len, ln_g3, ln_b3, l)

        def rw_state(st, n):
            return jnp.transpose(st.reshape(n, RW_HEAD, RW_HEADS, RW_HEAD), (0, 2, 1, 3))

        def shift_out(zl):
            lo = zl[:, Z_LORA:Z_LORA + LORA_W]
            return jnp.concatenate([zl[:, Z_RR:Z_RR + 3 * D_MIX], lo[:, 0:64], lo[:, 128:192], lo[:, 256:416]], axis=1)

        sh_p = shift_out(jnp.concatenate([z2[(bq + 1) * seq - 1:(bq + 1) * seq] for bq in range(nb)], axis=0))
        sh_s = shift_out(z2[rows_p:].reshape(db, t_len, NZ_2)[:, t_len - 1])

        s5p = _s5_tables(s5_lam_re[l], s5_lam_im[l], s5_log_step[l], s5_b_re[l], s5_b_im[l], s5_c_re[l], s5_c_im[l],
                         s5_d[l])
        s5p["glu_w"] = s5_glu_w
        s5p["glu_b"] = glu_b3
        o_s5_p, re_p, im_p = s5(z2, 0, nb, 6, s5_tt, s5p, l, zeros_s5, zeros_s5)
        o_s5_s, re_s, im_s = s5(z2, rows_p // t_len, db, 1, t_len, s5p, l,
                                state_s5_re[l].astype(F32).reshape(db, 1, S5_W),
                                state_s5_im[l].astype(F32).reshape(db, 1, S5_W))

        outs = [jnp.concatenate([a, b], axis=0) for a, b in
                ((o_att_p, o_att_s), (o_gla_p, o_gla_s), (o_rw_p, o_rw_s), (o_s5_p, o_s5_s))]
        merged = merge(outs, w_branch, zgate, l, tm=tm_mid, tn=512)
        x = matmul_rows(merged, w_out, l, tm=tm_big, tn=512, residual=x, name="out_proj")

        xn2, route = router(x, norm_moe3, wr3, br3, l, tm_small)
        blk_exp, slot_tok, slot_w, slot_of = _moe_plan(route, n_tok)
        hmid = moe_up(blk_exp, slot_tok, xn2, w_exp_gate, w_exp_up, l)
        yb = moe_down(blk_exp, hmid, w_exp_down, slot_w, l)
        x = moe_combine(slot_of, x, yb)

        vals = (k_p[:, :N_META], v_p[:, :N_META], k_p[:, -WINDOW:], v_p[:, -WINDOW:],
                gla_p, rw_state(st_p, nb), sh_p, re_p.reshape(nb, S5_GROUPS, S5_STATE),
                im_p.reshape(nb, S5_GROUPS, S5_STATE),
                new_wk, new_wv, gla_s, rw_state(st_s, db), sh_s,
                re_s.reshape(db, S5_GROUPS, S5_STATE), im_s.reshape(db, S5_GROUPS, S5_STATE))
        for i in range(16):
            rec[i].append(vals[i])

    y = rmsnorm_rows(x, norm_final.reshape(1, 1, D_MODEL).astype(F32), 0, F32, tm_small)
    y_prompt = jnp.stack([y[bq * seq + N_META:(bq + 1) * seq] for bq in range(nb)])
    y_sample = y[rows_p:].reshape(db, t_len, D_MODEL)
    return (y_prompt, y_sample, *[jnp.stack(r) for r in rec])
```

```python
import functools
import math

import numpy as np
import jax
import jax.numpy as jnp
from jax import lax
from jax.experimental import pallas as pl
from jax.experimental.pallas import tpu as pltpu

F32 = jnp.float32
BF16 = jnp.bfloat16

D_MODEL = 4096
N_META = 16
EPS = 1e-5
D_MIX = 1024
HEAD_DIM = 64
N_HEADS = 16
N_KV = 2
WINDOW = 128
ROPE_DIM = 16
ROPE_THETA = 500000.0
NEG_INF = -1e30
GLA_HEADS = 4
GLA_DK = 128
GLA_DV = 256
GLA_RANK = 16
GLA_NORMALIZER = 16.0
GLA_CHUNK = 64
RW_HEAD = 64
RW_HEADS = 16
RW_DECAY_RANK = 64
RW_A_RANK = 64
RW_G_RANK = 160
RW_IN = 3 * D_MIX + RW_DECAY_RANK + RW_A_RANK + RW_G_RANK
RW_LN_EPS = 64e-5
S5_GROUP = 16
S5_GROUPS = 64
S5_STATE = 64
N_GROUPS = 4
EXP_PER_GROUP = 8
N_EXPERTS = 32
TOP_K = 2
D_EXPERT = 512
MOE_BLOCK = 128
PAST_LEN = 16384
N_BRANCH = 4

LANES = 128
SUBLANES = 8
VMEM_BUDGET = 56 * 1024 * 1024

Z_Q = 0
Z_RR = 1024
Z_RK = 2048
Z_RV = 3072
Z_S5 = 4096
Z_GV = 5120
Z_GR = 6144
Z_LORA = 7168
Z_GQ = 7680
Z_GK = 8192
Z_AK = 8704
Z_AV = 8832
Z_GLR = 8960
NZ_MIX = 9216
LORA_W = 512

_SRC_Q, _SRC_K, _SRC_V, _SRC_GQ, _SRC_GK, _SRC_GV, _SRC_GR, _SRC_GLR = 0, 1024, 1152, 1280, 1792, 2304, 3328, 4352
_SRC_RW = 4368
_SRC_S5 = _SRC_RW + RW_IN
_SRC_GATE = _SRC_S5 + D_MIX


def _cparams(sem, vmem=VMEM_BUDGET):
    return pltpu.CompilerParams(dimension_semantics=sem, vmem_limit_bytes=int(vmem))


def _bdot(a, b):
    return jnp.dot(a.astype(BF16), b.astype(BF16), preferred_element_type=F32)


def _bdot_nt(a, b):
    return lax.dot_general(a.astype(BF16), b.astype(BF16), (((1,), (1,)), ((), ())), preferred_element_type=F32)


def _bdot_tn(a, b):
    return lax.dot_general(a.astype(BF16), b.astype(BF16), (((0,), (0,)), ((), ())), preferred_element_type=F32)


def _split2(x):
    hi = x.astype(BF16)
    lo = (x - hi.astype(F32)).astype(BF16)
    return hi, lo


def _dot2(x, w_bf16):
    hi, lo = _split2(x)
    return (jnp.dot(hi, w_bf16, preferred_element_type=F32) + jnp.dot(lo, w_bf16, preferred_element_type=F32))


def _sigmoid(x):
    return 1.0 / (1.0 + jnp.exp(-x))


def _silu(x):
    return x * _sigmoid(x)


def _softplus(x):
    return jnp.maximum(x, 0.0) + jnp.log(1.0 + jnp.exp(-jnp.abs(x)))


def _rmsnorm_body(x_ref, g_ref, o_ref):
    x = x_ref[...]
    y = x * lax.rsqrt(jnp.mean(x * x, axis=-1, keepdims=True) + EPS)
    o_ref[...] = (y * g_ref[...]).astype(o_ref.dtype)


def rmsnorm_rows(x, g3, layer, out_dtype, tm):
    m, d = x.shape
    return pl.pallas_call(
        _rmsnorm_body,
        grid=(m // tm,),
        in_specs=[pl.BlockSpec((tm, d), lambda i: (i, 0)),
                  pl.BlockSpec((None, 1, d), lambda i: (layer, 0, 0))],
        out_specs=pl.BlockSpec((tm, d), lambda i: (i, 0)),
        out_shape=jax.ShapeDtypeStruct((m, d), out_dtype),
        compiler_params=_cparams(("parallel",)),
        name="rmsnorm_rows",
    )(x, g3)


def _mm_body(a_ref, w_ref, o_ref):
    o_ref[...] = jnp.dot(a_ref[...], w_ref[...].astype(BF16), preferred_element_type=F32).astype(o_ref.dtype)


def _mm_res_body(a_ref, w_ref, r_ref, o_ref):
    o_ref[...] = r_ref[...] + jnp.dot(a_ref[...], w_ref[...].astype(BF16), preferred_element_type=F32)


def matmul_rows(a, w3, layer, *, tm, tn, n=None, residual=None, name="matmul_rows"):
    m, k = a.shape
    n = w3.shape[2] if n is None else n
    in_specs = [pl.BlockSpec((tm, k), lambda i, j: (i, 0)),
                pl.BlockSpec((None, k, tn), lambda i, j: (layer, 0, j))]
    args = [a, w3]
    body = _mm_body
    if residual is not None:
        in_specs.append(pl.BlockSpec((tm, tn), lambda i, j: (i, j)))
        args.append(residual)
        body = _mm_res_body
    return pl.pallas_call(
        body,
        grid=(m // tm, n // tn),
        in_specs=in_specs,
        out_specs=pl.BlockSpec((tm, tn), lambda i, j: (i, j)),
        out_shape=jax.ShapeDtypeStruct((m, n), F32),
        compiler_params=_cparams(("parallel", "arbitrary")),
        name=name,
    )(*args)


def _rope_tables(pos):
    half = ROPE_DIM // 2
    inv_freq = ROPE_THETA ** (-jnp.arange(half, dtype=F32) / half)
    ang = pos.astype(F32)[:, None] * inv_freq[None, :]
    cos, sin = jnp.cos(ang), jnp.sin(ang)
    n = pos.shape[0]
    ones = jnp.ones((n, HEAD_DIM - ROPE_DIM), F32)
    zeros = jnp.zeros((n, HEAD_DIM - ROPE_DIM), F32)
    c64 = jnp.concatenate([cos, cos, ones], axis=1)
    s64 = jnp.concatenate([-sin, sin, zeros], axis=1)
    return jnp.concatenate([c64, c64], axis=1), jnp.concatenate([s64, s64], axis=1)


def _rope(x, c, s):
    lane = lax.broadcasted_iota(jnp.int32, x.shape, 1) % HEAD_DIM
    partner = jnp.where(lane < ROPE_DIM // 2,
                        pltpu.roll(x, LANES - ROPE_DIM // 2, 1),
                        pltpu.roll(x, ROPE_DIM // 2, 1))
    return x * c + partner * s


def _dup_half(x, g):
    lane = lax.broadcasted_iota(jnp.int32, x.shape, 1)
    sw = pltpu.roll(x, HEAD_DIM, 1)
    if g == 0:
        return jnp.where(lane < HEAD_DIM, x, sw)
    return jnp.where(lane < HEAD_DIM, sw, x)


def _attn_frame(q_rows, cq, sq, kcat, vcat, valid, sink_ref, write):
    r = q_rows.shape[0]
    lane = lax.broadcasted_iota(jnp.int32, (r, LANES), 1)
    lane_v = lax.broadcasted_iota(jnp.int32, (3 * WINDOW, LANES), 1)
    for p in range(N_HEADS // 2):
        g = (2 * p) // (N_HEADS // N_KV)
        qp = _rope(q_rows[:, LANES * p:LANES * (p + 1)], cq, sq)
        acc = jnp.zeros((r, LANES), F32)
        for hh in range(2):
            h = 2 * p + hh
            in_half = (lane >= HEAD_DIM) if hh else (lane < HEAD_DIM)
            qm = jnp.where(in_half, qp, 0.0)
            logits = _bdot_nt(qm, kcat[g]) * (HEAD_DIM ** -0.5)
            logits = jnp.where(valid, logits, NEG_INF)
            sink = sink_ref[h]
            m = jnp.maximum(jnp.max(logits, axis=-1, keepdims=True), sink)
            pr = jnp.exp(logits - m)
            denom = jnp.sum(pr, axis=-1, keepdims=True) + jnp.exp(sink - m)
            v_half = jnp.where((lane_v >= HEAD_DIM) if hh else (lane_v < HEAD_DIM), vcat[g], 0.0)
            acc = acc + _bdot(pr / denom, v_half)
        write(p, acc)


def _frame_masks(r, no_prev):
    qi = lax.broadcasted_iota(jnp.int32, (r, 3 * WINDOW), 0)
    c = lax.broadcasted_iota(jnp.int32, (r, 3 * WINDOW), 1)
    prev_ok = (c < WINDOW) & (c > qi + no_prev * WINDOW)
    cur_ok = (c >= WINDOW) & (c < 2 * WINDOW) & (c - WINDOW <= qi)
    meta_ok = (c >= 2 * WINDOW) & (c < 2 * WINDOW + N_META)
    return prev_ok | cur_ok | meta_ok


def _attn_prompt_body(sink_ref, q_ref, k_ref, v_ref, c_ref, s_ref, o_ref, kr_ref):
    seq = q_ref.shape[0]
    nb = (seq - N_META) // WINDOW
    kr_ref[...] = _rope(k_ref[...], c_ref[...], s_ref[...])
    zpad = jnp.zeros((WINDOW - N_META, LANES), F32)
    kmeta = jnp.concatenate([kr_ref[0:N_META, :], zpad], axis=0)
    vmeta = jnp.concatenate([v_ref[0:N_META, :], zpad], axis=0)
    kmeta2 = [_dup_half(kmeta, g) for g in range(N_KV)]
    vmeta2 = [_dup_half(vmeta, g) for g in range(N_KV)]

    qi = lax.broadcasted_iota(jnp.int32, (N_META, 3 * WINDOW), 0)
    c = lax.broadcasted_iota(jnp.int32, (N_META, 3 * WINDOW), 1)
    valid_meta = (c >= 2 * WINDOW) & (c - 2 * WINDOW <= qi)
    kcat = [jnp.concatenate([kmeta2[g], kmeta2[g], kmeta2[g]], axis=0) for g in range(N_KV)]
    vcat = [jnp.concatenate([vmeta2[g], vmeta2[g], vmeta2[g]], axis=0) for g in range(N_KV)]

    def write_meta(p, val):
        o_ref[0:N_META, LANES * p:LANES * (p + 1)] = val

    _attn_frame(q_ref[0:N_META, :], c_ref[0:N_META, :], s_ref[0:N_META, :], kcat, vcat, valid_meta,
                sink_ref, write_meta)

    def frame(n, carry):
        r0 = pl.multiple_of(N_META + WINDOW * n, SUBLANES)
        p0 = pl.multiple_of(jnp.maximum(r0 - WINDOW, 0), SUBLANES)
        kprev, kcur = kr_ref[pl.ds(p0, WINDOW), :], kr_ref[pl.ds(r0, WINDOW), :]
        vprev, vcur = v_ref[pl.ds(p0, WINDOW), :], v_ref[pl.ds(r0, WINDOW), :]
        kc = [jnp.concatenate([_dup_half(kprev, g), _dup_half(kcur, g), kmeta2[g]], axis=0) for g in range(N_KV)]
        vc = [jnp.concatenate([_dup_half(vprev, g), _dup_half(vcur, g), vmeta2[g]], axis=0) for g in range(N_KV)]
        valid = _frame_masks(WINDOW, jnp.where(n == 0, 1, 0))

        def write(p, val):
            o_ref[pl.ds(r0, WINDOW), LANES * p:LANES * (p + 1)] = val

        _attn_frame(q_ref[pl.ds(r0, WINDOW), :], c_ref[pl.ds(r0, WINDOW), :], s_ref[pl.ds(r0, WINDOW), :],
                    kc, vc, valid, sink_ref, write)
        return carry

    lax.fori_loop(0, nb, frame, 0)


def attn_prompt(z, sinks, cos_t, sin_t, n_batch, seq):
    grid_spec = pltpu.PrefetchScalarGridSpec(
        num_scalar_prefetch=0,
        grid=(n_batch,),
        in_specs=[pl.BlockSpec(memory_space=pltpu.SMEM),
                  pl.BlockSpec((seq, D_MIX), lambda b: (b, Z_Q // D_MIX)),
                  pl.BlockSpec((seq, LANES), lambda b: (b, Z_AK // LANES)),
                  pl.BlockSpec((seq, LANES), lambda b: (b, Z_AV // LANES)),
                  pl.BlockSpec((seq, LANES), lambda b: (0, 0)),
                  pl.BlockSpec((seq, LANES), lambda b: (0, 0))],
        out_specs=[pl.BlockSpec((seq, D_MIX), lambda b: (b, 0)),
                   pl.BlockSpec((seq, LANES), lambda b: (b, 0))],
    )
    return pl.pallas_call(
        _attn_prompt_body,
        grid_spec=grid_spec,
        out_shape=[jax.ShapeDtypeStruct((n_batch * seq, D_MIX), F32),
                   jax.ShapeDtypeStruct((n_batch * seq, LANES), F32)],
        compiler_params=_cparams(("parallel",)),
        name="attn_prompt",
    )(sinks, z, z, z, cos_t, sin_t)


def _attn_sample_body(sink_ref, q_ref, k_ref, v_ref, c_ref, s_ref, mk_ref, mv_ref, wk_ref, wv_ref, o_ref, kr_ref,
                      *, n_seq, t_len):
    kr_ref[...] = _rope(k_ref[...], jnp.concatenate([c_ref[...]] * n_seq, axis=0),
                        jnp.concatenate([s_ref[...]] * n_seq, axis=0))
    zpad_m = jnp.zeros((WINDOW - N_META, LANES), F32)
    zpad_c = jnp.zeros((WINDOW - t_len, LANES), F32)
    valid = _frame_masks(t_len, 0)
    for s in range(n_seq):
        rows = slice(s * t_len, (s + 1) * t_len)
        kmeta = jnp.concatenate([mk_ref[s], zpad_m], axis=0)
        vmeta = jnp.concatenate([mv_ref[s], zpad_m], axis=0)
        kcur = jnp.concatenate([kr_ref[rows, :], zpad_c], axis=0)
        vcur = jnp.concatenate([v_ref[rows, :], zpad_c], axis=0)
        kprev, vprev = wk_ref[s], wv_ref[s]
        kc = [jnp.concatenate([_dup_half(kprev, g), _dup_half(kcur, g), _dup_half(kmeta, g)], axis=0)
              for g in range(N_KV)]
        vc = [jnp.concatenate([_dup_half(vprev, g), _dup_half(vcur, g), _dup_half(vmeta, g)], axis=0)
              for g in range(N_KV)]

        def write(p, val, rows=rows):
            o_ref[rows, LANES * p:LANES * (p + 1)] = val

        _attn_frame(q_ref[rows, :], c_ref[...], s_ref[...], kc, vc, valid, sink_ref, write)


def attn_sample(z, row0, sinks, cos_t, sin_t, meta_k, meta_v, win_k, win_v, n_seq_total, t_len, n_seq=8):
    rows = n_seq * t_len
    blk0 = row0 // rows
    grid_spec = pltpu.PrefetchScalarGridSpec(
        num_scalar_prefetch=0,
        grid=(n_seq_total // n_seq,),
        in_specs=[pl.BlockSpec(memory_space=pltpu.SMEM),
                  pl.BlockSpec((rows, D_MIX), lambda i: (blk0 + i, Z_Q // D_MIX)),
                  pl.BlockSpec((rows, LANES), lambda i: (blk0 + i, Z_AK // LANES)),
                  pl.BlockSpec((rows, LANES), lambda i: (blk0 + i, Z_AV // LANES)),
                  pl.BlockSpec((t_len, LANES), lambda i: (0, 0)),
                  pl.BlockSpec((t_len, LANES), lambda i: (0, 0)),
                  pl.BlockSpec((n_seq, N_META, LANES), lambda i: (i, 0, 0)),
                  pl.BlockSpec((n_seq, N_META, LANES), lambda i: (i, 0, 0)),
                  pl.BlockSpec((n_seq, WINDOW, LANES), lambda i: (i, 0, 0)),
                  pl.BlockSpec((n_seq, WINDOW, LANES), lambda i: (i, 0, 0))],
        out_specs=[pl.BlockSpec((rows, D_MIX), lambda i: (i, 0)),
                   pl.BlockSpec((rows, LANES), lambda i: (i, 0))],
    )
    return pl.pallas_call(
        functools.partial(_attn_sample_body, n_seq=n_seq, t_len=t_len),
        grid_spec=grid_spec,
        out_shape=[jax.ShapeDtypeStruct((n_seq_total * t_len, D_MIX), F32),
                   jax.ShapeDtypeStruct((n_seq_total * t_len, LANES), F32)],
        compiler_params=_cparams(("parallel",)),
        name="attn_sample",
    )(sinks, z, z, z, cos_t, sin_t, meta_k, meta_v, win_k, win_v)


def _gla_body(q_ref, k_ref, v_ref, r_ref, lr_ref, w2_ref, b_ref, norm_ref, h0_ref, o_ref, h_ref, ht_ref):
    seq = q_ref.shape[0]
    cs = GLA_CHUNK
    ht_ref[...] = h0_ref[...].T
    row = lax.broadcasted_iota(jnp.int32, (cs, cs), 0)
    col = lax.broadcasted_iota(jnp.int32, (cs, cs), 1)
    tril = row >= col
    tril_bf = jnp.where(tril, 1.0, 0.0).astype(BF16)

    def chunk(r0, nvalid):
        nload = min(cs, seq)

        def load(ref):
            x = ref[pl.ds(r0, nload), :]
            if nload < cs:
                x = jnp.concatenate([x, jnp.zeros((cs - nload, x.shape[1]), F32)], axis=0)
            return x

        q, k, v, gr, lr = load(q_ref), load(k_ref), load(v_ref), load(r_ref), load(lr_ref)
        pre = _bdot(lr, w2_ref[...]) + b_ref[...]
        la = (jnp.minimum(pre, 0.0) - jnp.log(1.0 + jnp.exp(-jnp.abs(pre)))) / GLA_NORMALIZER
        if nvalid < cs:
            live = lax.broadcasted_iota(jnp.int32, (cs, 1), 0) < nvalid
            la = jnp.where(live, la, 0.0)
            k = jnp.where(live, k, 0.0)
            v = jnp.where(live, v, 0.0)
        la_hi, la_lo = _split2(la)
        bc = (jnp.dot(tril_bf, la_hi, preferred_element_type=F32)
              + jnp.dot(tril_bf, la_lo, preferred_element_type=F32))
        b_last = bc[cs - 1:cs, :]
        q_in = q * jnp.exp(bc) * (GLA_DK ** -0.5)
        k_in = k * jnp.exp(-bc)
        k_out = k * jnp.exp(b_last - bc)
        att = jnp.where(tril, _bdot_nt(q_in, k_in), 0.0)
        ht = ht_ref[...]
        o = _bdot(att, v) + _bdot_nt(q_in, ht)
        ht_ref[...] = ht * jnp.exp(b_last) + _bdot_tn(v, k_out)
        o = o * lax.rsqrt(jnp.mean(o * o, axis=-1, keepdims=True) + EPS) * norm_ref[...]
        o = o * _silu(gr)
        o_ref[pl.ds(r0, nvalid), :] = o[0:nvalid, :]

    if seq < cs:
        chunk(0, seq)
    else:
        lead = seq % cs
        if lead:
            chunk(0, lead)

        def step(c, carry):
            chunk(pl.multiple_of(lead + cs * c, SUBLANES), cs)
            return carry

        lax.fori_loop(0, seq // cs, step, 0)
    h_ref[...] = ht_ref[...].T


def gla(z1, z2, blk0, n_seq, seq, w2p, b3, norm3, layer, h0):
    def zspec(width, col0):
        return pl.BlockSpec((seq, width), lambda s, h: (blk0 + s, col0 // width + h))

    return pl.pallas_call(
        _gla_body,
        grid=(n_seq, GLA_HEADS),
        in_specs=[zspec(GLA_DK, Z_GQ), zspec(GLA_DK, Z_GK), zspec(GLA_DV, Z_GV), zspec(GLA_DV, Z_GR),
                  pl.BlockSpec((seq, LANES), lambda s, h: (blk0 + s, Z_GLR // LANES)),
                  pl.BlockSpec((None, LANES, GLA_DK), lambda s, h: (layer, 0, h)),
                  pl.BlockSpec((None, 1, GLA_DK), lambda s, h: (layer, 0, h)),
                  pl.BlockSpec((None, 1, GLA_DV), lambda s, h: (layer, 0, 0)),
                  pl.BlockSpec((None, None, GLA_DK, GLA_DV), lambda s, h: (s, h, 0, 0))],
        out_specs=[pl.BlockSpec((seq, GLA_DV), lambda s, h: (s, h)),
                   pl.BlockSpec((None, None, GLA_DK, GLA_DV), lambda s, h: (s, h, 0, 0))],
        out_shape=[jax.ShapeDtypeStruct((n_seq * seq, D_MIX), F32),
                   jax.ShapeDtypeStruct((n_seq, GLA_HEADS, GLA_DK, GLA_DV), F32)],
        scratch_shapes=[pltpu.VMEM((GLA_DV, GLA_DK), F32)],
        compiler_params=_cparams(("parallel", "parallel")),
        name="gla",
    )(z1, z1, z1, z1, z2, w2p, b3, norm3, h0)


def _seg_ones(n, seg):
    r = lax.broadcasted_iota(jnp.int32, (n, n), 0) // seg
    c = lax.broadcasted_iota(jnp.int32, (n, n), 1) // seg
    return jnp.where(r == c, 1.0, 0.0).astype(BF16)


def _rw_pre_body(zr_ref, zk_ref, zv_ref, zl_ref, pr_ref, pk_ref, pv_ref, pl_ref,
                 mur_ref, muk_ref, muv_ref, mul_ref, w0_ref, w2_ref, a0_ref, a2_ref, g2_ref,
                 kk_ref, ka_ref, rk_ref,
                 r_out, w_out, k_out, v_out, kk_out, b_out, g_out, bonus_out):
    rows = zr_ref.shape[0]
    first = lax.broadcasted_iota(jnp.int32, (rows, 1), 0) == 0

    def shifted(z_ref, p_ref, mu_ref):
        z = z_ref[...]
        prev = jnp.where(first, p_ref[...], pltpu.roll(z, 1, 0))
        return z + (prev - z) * mu_ref[...]

    r = shifted(zr_ref, pr_ref, mur_ref)
    k = shifted(zk_ref, pk_ref, muk_ref)
    v = shifted(zv_ref, pv_ref, muv_ref)
    lo = shifted(zl_ref, pl_ref, mul_ref)
    zw, za, zg = lo[:, 0:LANES], lo[:, LANES:2 * LANES], lo[:, 2 * LANES:4 * LANES]
    w = -_softplus(-(w0_ref[...] + _bdot(jnp.tanh(zw), w2_ref[...]))) - 0.5
    decay = jnp.exp(-jnp.exp(w))
    a = _sigmoid(a0_ref[...] + _bdot(za, a2_ref[...]))
    g = _bdot(_sigmoid(zg), g2_ref[...])
    ones = _seg_ones(D_MIX, RW_HEAD)
    kkr = k * kk_ref[...]
    nrm = jnp.sqrt(_dot2(kkr * kkr, ones))
    kk = kkr / jnp.maximum(nrm, 1e-12)
    k2 = k * (1.0 + (a - 1.0) * ka_ref[...])
    r_out[...] = r
    w_out[...] = decay
    k_out[...] = k2
    v_out[...] = v
    kk_out[...] = kk
    b_out[...] = kk * a
    g_out[...] = g
    bonus_out[...] = _dot2(r * k2 * rk_ref[...], ones) * v


def rw_pre(z, prev, n_seq, tiles_per_seq, tt, blk0, lp, layer):
    nt = tiles_per_seq

    def zspec(width, col0):
        return pl.BlockSpec((tt, width), lambda s, j: (blk0 + s * nt + j, col0 // width))

    def pspec(width):
        return pl.BlockSpec((None, 1, width), lambda s, j: (s * nt + j, 0, 0))

    def lspec(width):
        return pl.BlockSpec((None, 1, width), lambda s, j: (layer, 0, 0))

    def wspec(rows_, cols_):
        return pl.BlockSpec((None, rows_, cols_), lambda s, j: (layer, 0, 0))

    tm_spec = pl.BlockSpec((tt, D_MIX), lambda s, j: (j, s))
    rm_spec = pl.BlockSpec((tt, D_MIX), lambda s, j: (s * nt + j, 0))
    tm_shape = jax.ShapeDtypeStruct((nt * tt, n_seq * D_MIX), F32)
    rm_shape = jax.ShapeDtypeStruct((n_seq * nt * tt, D_MIX), F32)
    return pl.pallas_call(
        _rw_pre_body,
        grid=(n_seq, nt),
        in_specs=[zspec(D_MIX, Z_RR), zspec(D_MIX, Z_RK), zspec(D_MIX, Z_RV), zspec(LORA_W, Z_LORA),
                  pspec(D_MIX), pspec(D_MIX), pspec(D_MIX), pspec(LORA_W),
                  lspec(D_MIX), lspec(D_MIX), lspec(D_MIX), lspec(LORA_W),
                  lspec(D_MIX), wspec(LANES, D_MIX), lspec(D_MIX), wspec(LANES, D_MIX), wspec(2 * LANES, D_MIX),
                  lspec(D_MIX), lspec(D_MIX), lspec(D_MIX)],
        out_specs=[tm_spec] * 6 + [rm_spec] * 2,
        out_shape=[tm_shape] * 6 + [rm_shape] * 2,
        compiler_params=_cparams(("parallel", "parallel")),
        name="rw_pre",
    )(z, z, z, z, prev["r"], prev["k"], prev["v"], prev["l"],
      lp["mu_r"], lp["mu_k"], lp["mu_v"], lp["mu_l"], lp["w0"], lp["w2"], lp["a0"], lp["a2"], lp["g2"],
      lp["kk"], lp["ka"], lp["rk"])


RW_GB = 4
RW_TILE = 256


def _rw_scan_body(r_ref, w_ref, k_ref, v_ref, kk_ref, b_ref, s0_ref, y_ref, s_out_ref, s_ref):
    tb = r_ref.shape[0]
    nj = D_MIX // RW_TILE
    hpt = RW_TILE // RW_HEAD

    @pl.when(pl.program_id(1) == 0)
    def _():
        s_ref[...] = s0_ref[...]

    ones4 = _seg_ones(RW_TILE, RW_HEAD)
    rr = lax.broadcasted_iota(jnp.int32, (RW_HEAD, RW_TILE), 0)
    cc = lax.broadcasted_iota(jnp.int32, (RW_HEAD, RW_TILE), 1)
    eye_rep = jnp.where(cc % RW_HEAD == rr, 1.0, 0.0).astype(BF16)
    hr = lax.broadcasted_iota(jnp.int32, (SUBLANES, RW_TILE), 0)
    hc = lax.broadcasted_iota(jnp.int32, (SUBLANES, RW_TILE), 1)
    head_rows = jnp.where(hc // RW_HEAD == hr, 1.0, 0.0)
    tiles = [(bi, j) for bi in range(RW_GB) for j in range(nj)]

    def step(t, carry):
        def row(ref, bi, j):
            return ref[pl.ds(t, 1), D_MIX * bi + RW_TILE * j:D_MIX * bi + RW_TILE * (j + 1)]

        p_list, vd_list, s_old = [], [], []
        for bi, j in tiles:
            s = s_ref[bi, :, RW_TILE * j:RW_TILE * (j + 1)]
            s_old.append(s)
            p_list.append((s * row(kk_ref, bi, j)).astype(BF16))
            vd_list.append(eye_rep * row(v_ref, bi, j).astype(BF16))
        both = jnp.dot(jnp.concatenate(p_list + vd_list, axis=0), ones4, preferred_element_type=F32)
        n = len(tiles) * RW_HEAD
        sa_all, vcol_all = both[0:n], both[n:2 * n]
        for idx, (bi, j) in enumerate(tiles):
            lanes = slice(RW_TILE * j, RW_TILE * (j + 1))
            rows = slice(RW_HEAD * idx, RW_HEAD * (idx + 1))
            s_new = (s_old[idx] * row(w_ref, bi, j) - sa_all[rows] * row(b_ref, bi, j)
                     + vcol_all[rows] * row(k_ref, bi, j))
            s_ref[bi, :, lanes] = s_new
            y8 = lax.dot_general((head_rows * row(r_ref, bi, j)).astype(BF16), s_new.astype(BF16),
                                 (((1,), (1,)), ((), ())), preferred_element_type=F32)
            y_ref[t, RW_HEADS * bi + hpt * j:RW_HEADS * bi + hpt * (j + 1), :] = y8[0:hpt, :]
        return carry

    lax.fori_loop(0, tb, step, 0, unroll=2)

    @pl.when(pl.program_id(1) == pl.num_programs(1) - 1)
    def _():
        s_out_ref[...] = s_ref[...]


def rw_scan(seqs, s0, tb):
    steps, width = seqs[0].shape
    n_seq = width // D_MIX
    gw = RW_GB * D_MIX
    in_spec = pl.BlockSpec((tb, gw), lambda g, i: (i, g))
    st_spec = pl.BlockSpec((RW_GB, RW_HEAD, D_MIX), lambda g, i: (g, 0, 0))
    return pl.pallas_call(
        _rw_scan_body,
        grid=(n_seq // RW_GB, steps // tb),
        in_specs=[in_spec] * 6 + [st_spec],
        out_specs=[pl.BlockSpec((tb, RW_GB * RW_HEADS, RW_HEAD), lambda g, i: (i, g, 0)), st_spec],
        out_shape=[jax.ShapeDtypeStruct((steps, n_seq * RW_HEADS, RW_HEAD), F32),
                   jax.ShapeDtypeStruct((n_seq, RW_HEAD, D_MIX), F32)],
        scratch_shapes=[pltpu.VMEM((RW_GB, RW_HEAD, D_MIX), F32)],
        compiler_params=_cparams(("parallel", "arbitrary")),
        name="rw_scan",
    )(*seqs, s0)


def _rw_post_body(y_ref, g_ref, bonus_ref, lng_ref, lnb_ref, o_ref):
    y = y_ref[...]
    ones = _seg_ones(D_MIX, RW_HEAD)
    mu = _dot2(y, ones) * (1.0 / RW_HEAD)
    d = y - mu
    var = _dot2(d * d, ones) * (1.0 / RW_HEAD)
    yn = d * lax.rsqrt(var + RW_LN_EPS) * lng_ref[...] + lnb_ref[...]
    o_ref[...] = (yn + bonus_ref[...]) * g_ref[...]


def rw_post(y_tm, g, bonus, n_seq, tiles_per_seq, tt, lng3, lnb3, layer):
    nt = tiles_per_seq
    rm_spec = pl.BlockSpec((tt, D_MIX), lambda s, j: (s * nt + j, 0))
    lspec = pl.BlockSpec((None, 1, D_MIX), lambda s, j: (layer, 0, 0))
    return pl.pallas_call(
        _rw_post_body,
        grid=(n_seq, nt),
        in_specs=[pl.BlockSpec((tt, D_MIX), lambda s, j: (j, s)), rm_spec, rm_spec, lspec, lspec],
        out_specs=rm_spec,
        out_shape=jax.ShapeDtypeStruct((n_seq * nt * tt, D_MIX), F32),
        compiler_params=_cparams(("parallel", "parallel")),
        name="rw_post",
    )(y_tm, g, bonus, lng3, lnb3)


S5_W = S5_GROUPS * S5_STATE
S5_KT = 256
S5_NT = S5_KT // S5_GROUP * S5_STATE


def _s5_body(u_ref, bbr_ref, bbi_ref, ccr_ref, cci_ref, d_ref, a_ref, apow_ref, gw_ref, gb_ref, h0r_ref, h0i_ref,
             o_ref, hr_out, hi_out, xr_ref, xi_ref, cr_ref, ci_ref):
    tt = u_ref.shape[0]
    nkt = D_MIX // S5_KT

    @pl.when(pl.program_id(1) == 0)
    def _():
        cr_ref[...] = h0r_ref[...]
        ci_ref[...] = h0i_ref[...]

    u = u_ref[...]
    for jt in range(nkt):
        ub = u[:, S5_KT * jt:S5_KT * (jt + 1)]
        xr_ref[:, S5_NT * jt:S5_NT * (jt + 1)] = _bdot(ub, bbr_ref[jt])
        xi_ref[:, S5_NT * jt:S5_NT * (jt + 1)] = _bdot(ub, bbi_ref[jt])

    sub = lax.broadcasted_iota(jnp.int32, (SUBLANES, S5_W), 0)
    a_re = [a_ref[2 * i:2 * i + 1, :] for i in range(3)]
    a_im = [a_ref[2 * i + 1:2 * i + 2, :] for i in range(3)]
    pw_re, pw_im = apow_ref[0], apow_ref[1]

    def group(gi, carry):
        r0 = pl.multiple_of(gi * SUBLANES, SUBLANES)
        hr, hi = xr_ref[pl.ds(r0, SUBLANES), :], xi_ref[pl.ds(r0, SUBLANES), :]
        for i, sh in enumerate((1, 2, 4)):
            keep = sub >= sh
            sr = jnp.where(keep, pltpu.roll(hr, sh, 0), 0.0)
            si = jnp.where(keep, pltpu.roll(hi, sh, 0), 0.0)
            hr, hi = hr + a_re[i] * sr - a_im[i] * si, hi + a_re[i] * si + a_im[i] * sr
        c_r, c_i = cr_ref[...], ci_ref[...]
        hr = hr + pw_re * c_r - pw_im * c_i
        hi = hi + pw_re * c_i + pw_im * c_r
        xr_ref[pl.ds(r0, SUBLANES), :] = hr
        xi_ref[pl.ds(r0, SUBLANES), :] = hi
        cr_ref[...] = hr[SUBLANES - 1:SUBLANES, :]
        ci_ref[...] = hi[SUBLANES - 1:SUBLANES, :]
        return carry

    lax.fori_loop(0, tt // SUBLANES, group, 0)

    ys = []
    for jt in range(nkt):
        hr = xr_ref[:, S5_NT * jt:S5_NT * (jt + 1)]
        hi = xi_ref[:, S5_NT * jt:S5_NT * (jt + 1)]
        ys.append(_bdot(hr, ccr_ref[jt]) - _bdot(hi, cci_ref[jt]))
    y = jnp.concatenate(ys, axis=1) + d_ref[...] * u
    ya = 0.5 * y * (1.0 + jnp.tanh(math.sqrt(2.0 / math.pi) * (y + 0.044715 * (y * y * y))))
    o_ref[...] = ya * _sigmoid(_bdot(ya, gw_ref[...]) + gb_ref[...])

    @pl.when(pl.program_id(1) == pl.num_programs(1) - 1)
    def _():
        hr_out[...] = cr_ref[...]
        hi_out[...] = ci_ref[...]


def s5(z, blk0, n_seq, tiles_per_seq, tt, sp, layer, h0r, h0i):
    nt = tiles_per_seq
    nkt = D_MIX // S5_KT

    def const3(shape):
        return pl.BlockSpec(shape, lambda s, j: (0,) * len(shape))

    st_spec = pl.BlockSpec((None, 1, S5_W), lambda s, j: (s, 0, 0))
    return pl.pallas_call(
        _s5_body,
        grid=(n_seq, nt),
        in_specs=[pl.BlockSpec((tt, D_MIX), lambda s, j: (blk0 + s * nt + j, Z_S5 // D_MIX)),
                  const3((nkt, S5_KT, S5_NT)), const3((nkt, S5_KT, S5_NT)),
                  const3((nkt, S5_NT, S5_KT)), const3((nkt, S5_NT, S5_KT)),
                  const3((1, D_MIX)), const3((6, S5_W)), const3((2, SUBLANES, S5_W)),
                  pl.BlockSpec((None, D_MIX, D_MIX), lambda s, j: (layer, 0, 0)),
                  pl.BlockSpec((None, 1, D_MIX), lambda s, j: (layer, 0, 0)),
                  st_spec, st_spec],
        out_specs=[pl.BlockSpec((tt, D_MIX), lambda s, j: (s * nt + j, 0)), st_spec, st_spec],
        out_shape=[jax.ShapeDtypeStruct((n_seq * nt * tt, D_MIX), F32),
                   jax.ShapeDtypeStruct((n_seq, 1, S5_W), F32),
                   jax.ShapeDtypeStruct((n_seq, 1, S5_W), F32)],
        scratch_shapes=[pltpu.VMEM((tt, S5_W), F32), pltpu.VMEM((tt, S5_W), F32),
                        pltpu.VMEM((1, S5_W), F32), pltpu.VMEM((1, S5_W), F32)],
        compiler_params=_cparams(("parallel", "arbitrary")),
        name="s5",
    )(z, sp["bbr"], sp["bbi"], sp["ccr"], sp["cci"], sp["d"], sp["a"], sp["apow"], sp["glu_w"], sp["glu_b"], h0r, h0i)


def _s5_tables(lam_re, lam_im, log_step, b_re, b_im, c_re, c_im, d):
    dt = jnp.exp(log_step.astype(F32))[:, None]
    lr = jnp.minimum(lam_re.astype(F32), -1e-4)
    li = lam_im.astype(F32)
    mag = jnp.exp(lr * dt)
    ab_re, ab_im = mag * jnp.cos(li * dt), mag * jnp.sin(li * dt)
    den = lr * lr + li * li
    f_re = ((ab_re - 1.0) * lr + ab_im * li) / den
    f_im = (ab_im * lr - (ab_re - 1.0) * li) / den
    bb_re = f_re[..., None] * b_re - f_im[..., None] * b_im
    bb_im = f_re[..., None] * b_im + f_im[..., None] * b_re
    gpt = S5_KT // S5_GROUP
    nkt = S5_GROUPS // gpt
    eye = jnp.eye(gpt, dtype=F32)

    def in_tiles(bb):
        t = bb.reshape(nkt, gpt, S5_STATE, S5_GROUP)
        t = jnp.einsum('jgph,gk->jghkp', t, eye)
        return t.reshape(nkt, S5_KT, S5_NT)

    def out_tiles(cc):
        t = cc.reshape(nkt, gpt, S5_GROUP, S5_STATE)
        t = jnp.einsum('jghp,gk->jgpkh', t, eye)
        return t.reshape(nkt, S5_NT, S5_KT)

    ar, ai = ab_re.reshape(1, S5_W), ab_im.reshape(1, S5_W)
    pows_r, pows_i = [ar], [ai]
    for _ in range(SUBLANES - 1):
        pr, pi = pows_r[-1], pows_i[-1]
        pows_r.append(pr * ar - pi * ai)
        pows_i.append(pr * ai + pi * ar)
    a_tab = jnp.concatenate([pows_r[0], pows_i[0], pows_r[1], pows_i[1], pows_r[3], pows_i[3]], axis=0)
    apow = jnp.stack([jnp.concatenate(pows_r, axis=0), jnp.concatenate(pows_i, axis=0)])
    return {"bbr": in_tiles(bb_re).astype(BF16), "bbi": in_tiles(bb_im).astype(BF16),
            "ccr": out_tiles(c_re.astype(F32)).astype(BF16), "cci": out_tiles(c_im.astype(F32)).astype(BF16), "d": d.reshape(1, D_MIX).astype(F32), "a": a_tab, "apow": apow}


def _merge_body(o0, o1, o2, o3, wb_ref, g0, g1, g2, g3, out_ref):
    acc = None
    for i, (o_ref, g_ref) in enumerate(((o0, g0), (o1, g1), (o2, g2), (o3, g3))):
        term = _sigmoid(g_ref[...]) * _bdot(o_ref[...], wb_ref[i])
        acc = term if acc is None else acc + term
    out_ref[...] = acc.astype(out_ref.dtype)


def merge(outs, w_branch, zgate, layer, *, tm, tn):
    m = outs[0].shape[0]
    nj = D_MODEL // tn
    o_spec = pl.BlockSpec((tm, D_MIX), lambda i, j: (i, 0))

    def gspec(b):
        return pl.BlockSpec((tm, tn), lambda i, j: (i, b * nj + j))

    return pl.pallas_call(
        _merge_body,
        grid=(m // tm, nj),
        in_specs=[o_spec] * 4 + [pl.BlockSpec((None, N_BRANCH, D_MIX, tn), lambda i, j: (layer, 0, 0, j))]
        + [gspec(b) for b in range(N_BRANCH)],
        out_specs=pl.BlockSpec((tm, tn), lambda i, j: (i, j)),
        out_shape=jax.ShapeDtypeStruct((m, D_MODEL), BF16),
        compiler_params=_cparams(("parallel", "arbitrary")),
        name="merge",
    )(*outs, w_branch, zgate, zgate, zgate, zgate)


def _router_body(x_ref, g_ref, wr_ref, br_ref, xn_ref, route_ref):
    x = x_ref[...]
    xn = x * lax.rsqrt(jnp.mean(x * x, axis=-1, keepdims=True) + EPS) * g_ref[...]
    xn_ref[...] = xn
    logits = _bdot(xn, wr_ref[...]) + br_ref[...]
    lane = lax.broadcasted_iota(jnp.int32, logits.shape, 1)
    big = jnp.int32(1 << 20)
    ninf = -jnp.inf
    lg = jnp.where(lane < N_GROUPS, logits, ninf)
    gmax = jnp.max(lg, axis=-1, keepdims=True)
    g_top = jnp.min(jnp.where(lg == gmax, lane, big), axis=-1, keepdims=True)
    pg = 1.0 / jnp.sum(jnp.exp(lg - gmax), axis=-1, keepdims=True)
    e_lane = lane - N_GROUPS
    in_group = (e_lane >= 0) & (e_lane < N_EXPERTS) & ((e_lane // EXP_PER_GROUP) == g_top)
    le = jnp.where(in_group, logits, ninf)
    m1 = jnp.max(le, axis=-1, keepdims=True)
    i1 = jnp.min(jnp.where(le == m1, lane, big), axis=-1, keepdims=True)
    le2 = jnp.where(lane == i1, ninf, le)
    m2 = jnp.max(le2, axis=-1, keepdims=True)
    i2 = jnp.min(jnp.where(le2 == m2, lane, big), axis=-1, keepdims=True)
    e2 = jnp.exp(m2 - m1)
    w1 = pg / (1.0 + e2)
    w2 = pg * e2 / (1.0 + e2)
    out = jnp.where(lane == 0, (i1 - N_GROUPS).astype(F32),
                    jnp.where(lane == 1, (i2 - N_GROUPS).astype(F32),
                              jnp.where(lane == 2, w1, jnp.where(lane == 3, w2, 0.0))))
    route_ref[...] = out


def router(x, g3, wr3, br3, layer, tm):
    m, d = x.shape
    return pl.pallas_call(
        _router_body,
        grid=(m // tm,),
        in_specs=[pl.BlockSpec((tm, d), lambda i: (i, 0)),
                  pl.BlockSpec((None, 1, d), lambda i: (layer, 0, 0)),
                  pl.BlockSpec((None, d, LANES), lambda i: (layer, 0, 0)),
                  pl.BlockSpec((None, 1, LANES), lambda i: (layer, 0, 0))],
        out_specs=[pl.BlockSpec((tm, d), lambda i: (i, 0)), pl.BlockSpec((tm, LANES), lambda i: (i, 0))],
        out_shape=[jax.ShapeDtypeStruct((m, d), F32), jax.ShapeDtypeStruct((m, LANES), F32)],
        compiler_params=_cparams(("parallel",)),
        name="moe_router",
    )(x, g3, wr3, br3)


def _gather_rows(src_hbm, idx_ref, base, dst, sem, n):
    for r in range(n):
        pltpu.make_async_copy(src_hbm.at[pl.ds(idx_ref[base + r], 1), :], dst.at[pl.ds(r, 1), :], sem).start()


def _wait_rows(src_hbm, dst, sem, n):
    for r in range(n):
        pltpu.make_async_copy(src_hbm.at[pl.ds(0, 1), :], dst.at[pl.ds(r, 1), :], sem).wait()


MOE_MB = 256


def _moe_up_body(exp_ref, tok_ref, nused_ref, x_hbm, w1_ref, w3_ref, h_ref, xbuf, sems):
    i = pl.program_id(0)
    n_used = nused_ref[0]
    slot = i % 2

    @pl.when(i == 0)
    def _():
        _gather_rows(x_hbm, tok_ref, 0, xbuf.at[0], sems.at[0], MOE_MB)

    @pl.when(i + 1 < n_used)
    def _():
        _gather_rows(x_hbm, tok_ref, (i + 1) * MOE_MB, xbuf.at[1 - slot], sems.at[1 - slot], MOE_MB)

    @pl.when(i < n_used)
    def _():
        _wait_rows(x_hbm, xbuf.at[slot], sems.at[slot], MOE_MB)
        xb = xbuf[slot].astype(BF16)
        a = jnp.dot(xb, w1_ref[...].astype(BF16), preferred_element_type=F32)
        b = jnp.dot(xb, w3_ref[...].astype(BF16), preferred_element_type=F32)
        h_ref[...] = _silu(a) * b

    @pl.when(i >= n_used)
    def _():
        h_ref[...] = jnp.zeros(h_ref.shape, F32)


def moe_up(blk_exp, slot_tok, n_used, xn, w1, w3, layer):
    n_blocks = blk_exp.shape[0]
    d = xn.shape[1]
    grid_spec = pltpu.PrefetchScalarGridSpec(
        num_scalar_prefetch=3,
        grid=(n_blocks,),
        in_specs=[pl.BlockSpec(memory_space=pl.ANY),
                  pl.BlockSpec((None, None, d, D_EXPERT), lambda i, e, t, u: (layer, e[i], 0, 0)),
                  pl.BlockSpec((None, None, d, D_EXPERT), lambda i, e, t, u: (layer, e[i], 0, 0))],
        out_specs=pl.BlockSpec((MOE_MB, D_EXPERT), lambda i, e, t, u: (i, 0)),
        scratch_shapes=[pltpu.VMEM((2, MOE_MB, d), F32), pltpu.SemaphoreType.DMA((2,))],
    )
    return pl.pallas_call(
        _moe_up_body,
        grid_spec=grid_spec,
        out_shape=jax.ShapeDtypeStruct((n_blocks * MOE_MB, D_EXPERT), F32),
        compiler_params=_cparams(("arbitrary",)),
        name="moe_up",
    )(blk_exp, slot_tok, n_used, xn, w1, w3)


def _moe_down_body(exp_ref, nused_ref, h_ref, w2_ref, sw_ref, y_ref):
    i = pl.program_id(0)

    @pl.when(i < nused_ref[0])
    def _():
        y_ref[...] = (jnp.dot(h_ref[...].astype(BF16), w2_ref[...].astype(BF16), preferred_element_type=F32)
                      * sw_ref[...])

    @pl.when(i >= nused_ref[0])
    def _():
        y_ref[...] = jnp.zeros(y_ref.shape, F32)


def moe_down(blk_exp, n_used, h, w2, slot_w, layer):
    n_blocks = blk_exp.shape[0]
    d = w2.shape[3]
    grid_spec = pltpu.PrefetchScalarGridSpec(
        num_scalar_prefetch=2,
        grid=(n_blocks,),
        in_specs=[pl.BlockSpec((MOE_MB, D_EXPERT), lambda i, e, u: (i, 0)),
                  pl.BlockSpec((None, None, D_EXPERT, d), lambda i, e, u: (layer, e[i], 0, 0)),
                  pl.BlockSpec((MOE_MB, 1), lambda i, e, u: (i, 0))],
        out_specs=pl.BlockSpec((MOE_MB, d), lambda i, e, u: (i, 0)),
    )
    return pl.pallas_call(
        _moe_down_body,
        grid_spec=grid_spec,
        out_shape=jax.ShapeDtypeStruct((n_blocks * MOE_MB, d), F32),
        compiler_params=_cparams(("arbitrary",)),
        name="moe_down",
    )(blk_exp, n_used, h, w2, slot_w)


MOE_TC = 64


def _moe_combine_body(slot_ref, x_ref, y_hbm, o_ref, ybuf, sems):
    i = pl.program_id(0)
    nstep = pl.num_programs(0)
    slot = i % 2
    n = TOP_K * MOE_TC

    @pl.when(i == 0)
    def _():
        _gather_rows(y_hbm, slot_ref, 0, ybuf.at[0], sems.at[0], n)

    @pl.when(i + 1 < nstep)
    def _():
        _gather_rows(y_hbm, slot_ref, (i + 1) * n, ybuf.at[1 - slot], sems.at[1 - slot], n)

    _wait_rows(y_hbm, ybuf.at[slot], sems.at[slot], n)
    o_ref[...] = x_ref[...] + (ybuf[slot, 0:MOE_TC, :] + ybuf[slot, MOE_TC:n, :])


def moe_combine(slot_of, x, yb):
    m, d = x.shape
    grid_spec = pltpu.PrefetchScalarGridSpec(
        num_scalar_prefetch=1,
        grid=(m // MOE_TC,),
        in_specs=[pl.BlockSpec((MOE_TC, d), lambda i, s: (i, 0)),
                  pl.BlockSpec(memory_space=pl.ANY)],
        out_specs=pl.BlockSpec((MOE_TC, d), lambda i, s: (i, 0)),
        scratch_shapes=[pltpu.VMEM((2, TOP_K * MOE_TC, d), F32), pltpu.SemaphoreType.DMA((2,))],
    )
    return pl.pallas_call(
        _moe_combine_body,
        grid_spec=grid_spec,
        out_shape=jax.ShapeDtypeStruct((m, d), F32),
        compiler_params=_cparams(("arbitrary",)),
        name="moe_combine",
    )(slot_of, x, yb)


def _moe_plan(route, n_tok):
    e_idx = route[:, 0:TOP_K].astype(jnp.int32)
    w = route[:, TOP_K:2 * TOP_K]
    n_assign = n_tok * TOP_K
    e_flat = e_idx.reshape(n_assign)
    onehot = (e_flat[:, None] == jnp.arange(N_EXPERTS, dtype=jnp.int32)[None, :]).astype(jnp.int32)
    csum = jnp.cumsum(onehot, axis=0)
    counts = csum[-1]
    rank = jnp.sum((csum - onehot) * onehot, axis=1)
    padded = (counts + MOE_MB - 1) // MOE_MB * MOE_MB
    pad_end = jnp.cumsum(padded)
    pad_start = pad_end - padded
    dest = pad_start[e_flat] + rank
    n_blocks = -(-(n_assign + N_EXPERTS * (MOE_MB - 1)) // MOE_MB)
    n_slots = n_blocks * MOE_MB
    tok_flat = jnp.repeat(jnp.arange(n_tok, dtype=jnp.int32), TOP_K)
    packed = jnp.stack([tok_flat.astype(F32), w.reshape(n_assign)], axis=1)
    slots = jnp.zeros((n_slots, 2), F32).at[dest].set(packed)
    slot_tok = slots[:, 0].astype(jnp.int32)
    slot_w = slots[:, 1:2]
    n_used = (pad_end[N_EXPERTS - 1] // MOE_MB).astype(jnp.int32)
    blk = jnp.minimum(jnp.arange(n_blocks, dtype=jnp.int32), n_used - 1)
    blk_exp = jnp.sum((pad_end[None, :] <= (blk * MOE_MB)[:, None]).astype(jnp.int32), axis=1).astype(jnp.int32)
    d2 = dest.reshape(n_tok // MOE_TC, MOE_TC, TOP_K)
    slot_of = jnp.transpose(d2, (0, 2, 1)).reshape(-1)
    return blk_exp, slot_tok, slot_w, slot_of, n_used.reshape(1)


def kernel(x_prompt, x_sample, cache_attn_meta_k, cache_attn_meta_v, cache_attn_win_k, cache_attn_win_v, state_gla, state_rwkv, state_rwkv_shift, state_s5_re, state_s5_im, meta_tokens, norm_mix, w_in, w_gla_gate, b_gla_gate, gla_norm, attn_sinks, rw_mu, rw_w0, rw_w2, rw_a0, rw_a2, rw_g2, rw_kk, rw_ka, rw_rk, rw_ln_g, rw_ln_b, s5_lam_re, s5_lam_im, s5_log_step, s5_b_re, s5_b_im, s5_c_re, s5_c_im, s5_d, s5_glu_w, s5_glu_b, w_branch, w_out, norm_moe, w_router_group, b_router_group, w_router_expert, b_router_expert, w_exp_gate, w_exp_up, w_exp_down, norm_final):
    depth = w_in.shape[0]
    nb, seq_p = x_prompt.shape[0], x_prompt.shape[1]
    db, t_len = x_sample.shape[0], x_sample.shape[1]
    seq = seq_p + N_META
    rows_p = nb * seq
    rows_s = db * t_len
    n_tok = rows_p + rows_s
    wb_len = cache_attn_win_k.shape[2]

    tm_big = n_tok // 7
    tm_mid = n_tok // 14
    tm_small = n_tok // 19
    rw_nt = 6
    rw_tt = seq // rw_nt
    rw_tb = seq // 43
    s5_tt = seq // 6

    pieces = []
    for bq in range(nb):
        pieces += [meta_tokens.astype(F32), x_prompt[bq]]
    x = jnp.concatenate(pieces + [x_sample.reshape(rows_s, D_MODEL)], axis=0)

    cos_p, sin_p = _rope_tables(jnp.arange(seq, dtype=jnp.int32))
    cos_s, sin_s = _rope_tables(PAST_LEN + jnp.arange(t_len, dtype=jnp.int32))

    def r3(a):
        return a.reshape(depth, 1, a.shape[-1]).astype(F32)

    norm_mix3, norm_moe3 = r3(norm_mix), r3(norm_moe)
    b_gla3, gla_norm3 = r3(b_gla_gate), r3(gla_norm)
    w2p = jnp.pad(w_gla_gate, ((0, 0), (0, LANES - GLA_RANK), (0, 0)))

    def lora_cols(a):
        c = 3 * D_MIX
        zw, za, zg = a[..., c:c + 64], a[..., c + 64:c + 128], a[..., c + 128:c + 288]
        pad = lambda t, n: jnp.pad(t, [(0, 0)] * (t.ndim - 1) + [(0, n - t.shape[-1])])
        lo = jnp.concatenate([pad(zw, LANES), pad(za, LANES), pad(zg, 2 * LANES)], axis=-1)
        return a[..., 0:D_MIX], a[..., D_MIX:2 * D_MIX], a[..., 2 * D_MIX:c], lo

    mu_r, mu_k, mu_v, mu_l = lora_cols(rw_mu)
    rw_params = {
        "mu_r": r3(mu_r), "mu_k": r3(mu_k), "mu_v": r3(mu_v), "mu_l": r3(mu_l),
        "w0": r3(rw_w0), "a0": r3(rw_a0), "kk": r3(rw_kk), "ka": r3(rw_ka),
        "rk": r3(rw_rk.reshape(depth, D_MIX)),
        "w2": jnp.pad(rw_w2, ((0, 0), (0, LANES - RW_DECAY_RANK), (0, 0))),
        "a2": jnp.pad(rw_a2, ((0, 0), (0, LANES - RW_A_RANK), (0, 0))),
        "g2": jnp.pad(rw_g2, ((0, 0), (0, 2 * LANES - RW_G_RANK), (0, 0))),
    }
    ln_g3, ln_b3 = r3(rw_ln_g), r3(rw_ln_b)
    glu_b3 = r3(s5_glu_b)
    wr3 = jnp.pad(jnp.concatenate([w_router_group, w_router_expert], axis=2).astype(F32),
                  ((0, 0), (0, 0), (0, LANES - N_GROUPS - N_EXPERTS)))
    br3 = jnp.pad(jnp.concatenate([b_router_group, b_router_expert], axis=1).astype(F32),
                  ((0, 0), (0, LANES - N_GROUPS - N_EXPERTS))).reshape(depth, 1, LANES)

    rec = [[] for _ in range(16)]
    zeros_gla = jnp.zeros((nb, GLA_HEADS, GLA_DK, GLA_DV), F32)
    zeros_rw = jnp.zeros((nb, RW_HEAD, D_MIX), F32)
    zeros_s5 = jnp.zeros((nb, 1, S5_W), F32)

    for l in range(depth):
        zcol = lambda n: jnp.zeros((D_MODEL, n), F32)
        wcols = lambda a, n: lax.slice(w_in, (l, 0, a), (l + 1, D_MODEL, a + n))[0]
        rw0 = _SRC_RW
        w_mix = jnp.concatenate([
            wcols(_SRC_Q, 1024), wcols(rw0, 3072), wcols(_SRC_S5, 1024), wcols(_SRC_GV, 2048),
            wcols(rw0 + 3072, 64), zcol(64), wcols(rw0 + 3136, 64), zcol(64), wcols(rw0 + 3200, 160), zcol(96),
            wcols(_SRC_GQ, 1024), wcols(_SRC_K, 256),
            wcols(_SRC_GLR, 16), zcol(112), zcol(NZ_MIX - Z_GLR - LANES)], axis=1)[None]
        w_gate = wcols(_SRC_GATE, N_BRANCH * D_MODEL)[None]
        xn = rmsnorm_rows(x, norm_mix3, l, BF16, tm_small)
        z = matmul_rows(xn, w_mix, 0, tm=tm_big, tn=512, name="in_proj_mix")
        z1 = z2 = z
        zgate = matmul_rows(xn, w_gate, 0, tm=tm_big, tn=512, name="in_proj_gate")

        sinks = attn_sinks[l].astype(F32)
        o_att_p, kr_p = attn_prompt(z1, sinks, cos_p, sin_p, nb, seq)
        mk = cache_attn_meta_k[l].reshape(db, N_META, LANES)
        mv = cache_attn_meta_v[l].reshape(db, N_META, LANES)
        wk = cache_attn_win_k[l].reshape(db, wb_len, LANES)
        wv = cache_attn_win_v[l].reshape(db, wb_len, LANES)
        o_att_s, kr_s = attn_sample(z1, rows_p, sinks, cos_s, sin_s, mk, mv, wk, wv, db, t_len)
        v_p = z1[:rows_p, Z_AV:Z_AV + LANES].reshape(nb, seq, N_KV, HEAD_DIM)
        k_p = kr_p.reshape(nb, seq, N_KV, HEAD_DIM)
        v_s = z1[rows_p:, Z_AV:Z_AV + LANES].reshape(db, t_len, N_KV, HEAD_DIM)
        k_s = kr_s.reshape(db, t_len, N_KV, HEAD_DIM)
        new_wk = jnp.concatenate([cache_attn_win_k[l].astype(F32), k_s], axis=1)[:, -wb_len:]
        new_wv = jnp.concatenate([cache_attn_win_v[l].astype(F32), v_s], axis=1)[:, -wb_len:]

        o_gla_p, gla_p = gla(z1, z2, 0, nb, seq, w2p, b_gla3, gla_norm3, l, zeros_gla)
        o_gla_s, gla_s = gla(z1, z2, rows_p // t_len, db, t_len, w2p, b_gla3, gla_norm3, l,
                             state_gla[l].astype(F32))

        def prev_rows(col0, width):
            rows_ = []
            for bq in range(nb):
                for jq in range(rw_nt):
                    r_ = bq * seq + jq * rw_tt
                    rows_.append(jnp.zeros((1, width), F32) if jq == 0 else z2[r_ - 1:r_, col0:col0 + width])
            return jnp.stack(rows_)

        prev_p = {"r": prev_rows(Z_RR, D_MIX), "k": prev_rows(Z_RK, D_MIX), "v": prev_rows(Z_RV, D_MIX),
                  "l": prev_rows(Z_LORA, LORA_W)}
        sr, sk, sv, sl = lora_cols(state_rwkv_shift[l].astype(F32))
        prev_s = {"r": sr[:, None], "k": sk[:, None], "v": sv[:, None], "l": sl[:, None]}

        pre_p = rw_pre(z2, prev_p, nb, rw_nt, rw_tt, 0, rw_params, l)
        pre_s = rw_pre(z2, prev_s, db, 1, t_len, rows_p // t_len, rw_params, l)
        y_p, st_p = rw_scan(pre_p[:6], zeros_rw, rw_tb)
        s0_s = jnp.transpose(state_rwkv[l].astype(F32), (0, 2, 1, 3)).reshape(db, RW_HEAD, D_MIX)
        y_s, st_s = rw_scan(pre_s[:6], s0_s, t_len)
        o_rw_p = rw_post(y_p.reshape(seq, nb * D_MIX), pre_p[6], pre_p[7], nb, rw_nt, rw_tt, ln_g3, ln_b3, l)
        o_rw_s = rw_post(y_s.reshape(t_len, db * D_MIX), pre_s[6], pre_s[7], db, 1, t_len, ln_g3, ln_b3, l)

        def rw_state(st, n):
            return jnp.transpose(st.reshape(n, RW_HEAD, RW_HEADS, RW_HEAD), (0, 2, 1, 3))

        def shift_out(zl):
            lo = zl[:, Z_LORA:Z_LORA + LORA_W]
            return jnp.concatenate([zl[:, Z_RR:Z_RR + 3 * D_MIX], lo[:, 0:64], lo[:, 128:192], lo[:, 256:416]], axis=1)

        sh_p = shift_out(jnp.concatenate([z2[(bq + 1) * seq - 1:(bq + 1) * seq] for bq in range(nb)], axis=0))
        sh_s = shift_out(z2[rows_p:].reshape(db, t_len, NZ_MIX)[:, t_len - 1])

        s5p = _s5_tables(s5_lam_re[l], s5_lam_im[l], s5_log_step[l], s5_b_re[l], s5_b_im[l], s5_c_re[l], s5_c_im[l],
                         s5_d[l])
        s5p["glu_w"] = s5_glu_w
        s5p["glu_b"] = glu_b3
        o_s5_p, re_p, im_p = s5(z2, 0, nb, 6, s5_tt, s5p, l, zeros_s5, zeros_s5)
        o_s5_s, re_s, im_s = s5(z2, rows_p // t_len, db, 1, t_len, s5p, l,
                                state_s5_re[l].astype(F32).reshape(db, 1, S5_W),
                                state_s5_im[l].astype(F32).reshape(db, 1, S5_W))

        outs = [jnp.concatenate([a, b], axis=0).astype(BF16) for a, b in
                ((o_att_p, o_att_s), (o_gla_p, o_gla_s), (o_rw_p, o_rw_s), (o_s5_p, o_s5_s))]
        merged = merge(outs, w_branch, zgate, l, tm=tm_big, tn=256)
        x = matmul_rows(merged, w_out, l, tm=tm_big, tn=512, residual=x, name="out_proj")

        xn2, route = router(x, norm_moe3, wr3, br3, l, tm_small)
        blk_exp, slot_tok, slot_w, slot_of, n_used = _moe_plan(route, n_tok)
        hmid = moe_up(blk_exp, slot_tok, n_used, xn2, w_exp_gate, w_exp_up, l)
        yb = moe_down(blk_exp, n_used, hmid, w_exp_down, slot_w, l)
        x = moe_combine(slot_of, x, yb)

        vals = (k_p[:, :N_META], v_p[:, :N_META], k_p[:, -WINDOW:], v_p[:, -WINDOW:],
                gla_p, rw_state(st_p, nb), sh_p, re_p.reshape(nb, S5_GROUPS, S5_STATE),
                im_p.reshape(nb, S5_GROUPS, S5_STATE),
                new_wk, new_wv, gla_s, rw_state(st_s, db), sh_s,
                re_s.reshape(db, S5_GROUPS, S5_STATE), im_s.reshape(db, S5_GROUPS, S5_STATE))
        for i in range(16):
            rec[i].append(vals[i])

    y = rmsnorm_rows(x, norm_final.reshape(1, 1, D_MODEL).astype(F32), 0, F32, tm_small)
    y_prompt = jnp.stack([y[bq * seq + N_META:(bq + 1) * seq] for bq in range(nb)])
    y_sample = y[rows_p:].reshape(db, t_len, D_MODEL)
    return (y_prompt, y_sample, *[jnp.stack(r) for r in rec])
```

```python
import functools
import math

import numpy as np
import jax
import jax.numpy as jnp
from jax import lax
from jax.experimental import pallas as pl
from jax.experimental.pallas import tpu as pltpu

F32 = jnp.float32
BF16 = jnp.bfloat16

D_MODEL = 4096
N_META = 16
EPS = 1e-5
D_MIX = 1024
HEAD_DIM = 64
N_HEADS = 16
N_KV = 2
WINDOW = 128
ROPE_DIM = 16
ROPE_THETA = 500000.0
NEG_INF = -1e30
GLA_HEADS = 4
GLA_DK = 128
GLA_DV = 256
GLA_RANK = 16
GLA_NORMALIZER = 16.0
GLA_CHUNK = 64
RW_HEAD = 64
RW_HEADS = 16
RW_DECAY_RANK = 64
RW_A_RANK = 64
RW_G_RANK = 160
RW_IN = 3 * D_MIX + RW_DECAY_RANK + RW_A_RANK + RW_G_RANK
RW_LN_EPS = 64e-5
S5_GROUP = 16
S5_GROUPS = 64
S5_STATE = 64
N_GROUPS = 4
EXP_PER_GROUP = 8
N_EXPERTS = 32
TOP_K = 2
D_EXPERT = 512
MOE_BLOCK = 128
PAST_LEN = 16384
N_BRANCH = 4

LANES = 128
SUBLANES = 8
VMEM_BUDGET = 56 * 1024 * 1024

Z_Q = 0
Z_RR = 1024
Z_RK = 2048
Z_RV = 3072
Z_S5 = 4096
Z_GV = 5120
Z_GR = 6144
Z_LORA = 7168
Z_GQ = 7680
Z_GK = 8192
Z_AK = 8704
Z_AV = 8832
Z_GLR = 8960
NZ_MIX = 9216
LORA_W = 512

_SRC_Q, _SRC_K, _SRC_V, _SRC_GQ, _SRC_GK, _SRC_GV, _SRC_GR, _SRC_GLR = 0, 1024, 1152, 1280, 1792, 2304, 3328, 4352
_SRC_RW = 4368
_SRC_S5 = _SRC_RW + RW_IN
_SRC_GATE = _SRC_S5 + D_MIX


def _cparams(sem, vmem=VMEM_BUDGET):
    return pltpu.CompilerParams(dimension_semantics=sem, vmem_limit_bytes=int(vmem))


def _bdot(a, b):
    return jnp.dot(a.astype(BF16), b.astype(BF16), preferred_element_type=F32)


def _bdot_nt(a, b):
    return lax.dot_general(a.astype(BF16), b.astype(BF16), (((1,), (1,)), ((), ())), preferred_element_type=F32)


def _bdot_tn(a, b):
    return lax.dot_general(a.astype(BF16), b.astype(BF16), (((0,), (0,)), ((), ())), preferred_element_type=F32)


def _split2(x):
    hi = x.astype(BF16)
    lo = (x - hi.astype(F32)).astype(BF16)
    return hi, lo


def _dot2(x, w_bf16):
    hi, lo = _split2(x)
    return (jnp.dot(hi, w_bf16, preferred_element_type=F32) + jnp.dot(lo, w_bf16, preferred_element_type=F32))


def _sigmoid(x):
    return 1.0 / (1.0 + jnp.exp(-x))


def _silu(x):
    return x * _sigmoid(x)


def _softplus(x):
    return jnp.maximum(x, 0.0) + jnp.log(1.0 + jnp.exp(-jnp.abs(x)))


def _rmsnorm_body(x_ref, g_ref, o_ref):
    x = x_ref[...]
    y = x * lax.rsqrt(jnp.mean(x * x, axis=-1, keepdims=True) + EPS)
    o_ref[...] = (y * g_ref[...]).astype(o_ref.dtype)


def rmsnorm_rows(x, g3, layer, out_dtype, tm):
    m, d = x.shape
    return pl.pallas_call(
        _rmsnorm_body,
        grid=(m // tm,),
        in_specs=[pl.BlockSpec((tm, d), lambda i: (i, 0)),
                  pl.BlockSpec((None, 1, d), lambda i: (layer, 0, 0))],
        out_specs=pl.BlockSpec((tm, d), lambda i: (i, 0)),
        out_shape=jax.ShapeDtypeStruct((m, d), out_dtype),
        compiler_params=_cparams(("parallel",)),
        name="rmsnorm_rows",
    )(x, g3)


def _mm_body(a_ref, w_ref, o_ref):
    o_ref[...] = jnp.dot(a_ref[...], w_ref[...].astype(BF16), preferred_element_type=F32).astype(o_ref.dtype)


def _mm_res_body(a_ref, w_ref, r_ref, o_ref):
    o_ref[...] = r_ref[...] + jnp.dot(a_ref[...], w_ref[...].astype(BF16), preferred_element_type=F32)


def matmul_rows(a, w3, layer, *, tm, tn, n=None, residual=None, name="matmul_rows"):
    m, k = a.shape
    n = w3.shape[2] if n is None else n
    in_specs = [pl.BlockSpec((tm, k), lambda i, j: (i, 0)),
                pl.BlockSpec((None, k, tn), lambda i, j: (layer, 0, j))]
    args = [a, w3]
    body = _mm_body
    if residual is not None:
        in_specs.append(pl.BlockSpec((tm, tn), lambda i, j: (i, j)))
        args.append(residual)
        body = _mm_res_body
    return pl.pallas_call(
        body,
        grid=(m // tm, n // tn),
        in_specs=in_specs,
        out_specs=pl.BlockSpec((tm, tn), lambda i, j: (i, j)),
        out_shape=jax.ShapeDtypeStruct((m, n), F32),
        compiler_params=_cparams(("parallel", "arbitrary")),
        name=name,
    )(*args)


def _rope_tables(pos):
    half = ROPE_DIM // 2
    inv_freq = ROPE_THETA ** (-jnp.arange(half, dtype=F32) / half)
    ang = pos.astype(F32)[:, None] * inv_freq[None, :]
    cos, sin = jnp.cos(ang), jnp.sin(ang)
    n = pos.shape[0]
    ones = jnp.ones((n, HEAD_DIM - ROPE_DIM), F32)
    zeros = jnp.zeros((n, HEAD_DIM - ROPE_DIM), F32)
    c64 = jnp.concatenate([cos, cos, ones], axis=1)
    s64 = jnp.concatenate([-sin, sin, zeros], axis=1)
    return jnp.concatenate([c64, c64], axis=1), jnp.concatenate([s64, s64], axis=1)


def _rope(x, c, s):
    lane = lax.broadcasted_iota(jnp.int32, x.shape, 1) % HEAD_DIM
    partner = jnp.where(lane < ROPE_DIM // 2,
                        pltpu.roll(x, LANES - ROPE_DIM // 2, 1),
                        pltpu.roll(x, ROPE_DIM // 2, 1))
    return x * c + partner * s


def _dup_half(x, g):
    lane = lax.broadcasted_iota(jnp.int32, x.shape, 1)
    sw = pltpu.roll(x, HEAD_DIM, 1)
    if g == 0:
        return jnp.where(lane < HEAD_DIM, x, sw)
    return jnp.where(lane < HEAD_DIM, sw, x)


def _attn_frame(q_rows, cq, sq, kcat, vcat, valid, sink_ref, write):
    r = q_rows.shape[0]
    lane = lax.broadcasted_iota(jnp.int32, (r, LANES), 1)
    lane_v = lax.broadcasted_iota(jnp.int32, (3 * WINDOW, LANES), 1)
    for p in range(N_HEADS // 2):
        g = (2 * p) // (N_HEADS // N_KV)
        qp = _rope(q_rows[:, LANES * p:LANES * (p + 1)], cq, sq)
        acc = jnp.zeros((r, LANES), F32)
        for hh in range(2):
            h = 2 * p + hh
            in_half = (lane >= HEAD_DIM) if hh else (lane < HEAD_DIM)
            qm = jnp.where(in_half, qp, 0.0)
            logits = _bdot_nt(qm, kcat[g]) * (HEAD_DIM ** -0.5)
            logits = jnp.where(valid, logits, NEG_INF)
            sink = sink_ref[h]
            m = jnp.maximum(jnp.max(logits, axis=-1, keepdims=True), sink)
            pr = jnp.exp(logits - m)
            denom = jnp.sum(pr, axis=-1, keepdims=True) + jnp.exp(sink - m)
            v_half = jnp.where((lane_v >= HEAD_DIM) if hh else (lane_v < HEAD_DIM), vcat[g], 0.0)
            acc = acc + _bdot(pr / denom, v_half)
        write(p, acc)


def _frame_masks(r, no_prev):
    qi = lax.broadcasted_iota(jnp.int32, (r, 3 * WINDOW), 0)
    c = lax.broadcasted_iota(jnp.int32, (r, 3 * WINDOW), 1)
    prev_ok = (c < WINDOW) & (c > qi + no_prev * WINDOW)
    cur_ok = (c >= WINDOW) & (c < 2 * WINDOW) & (c - WINDOW <= qi)
    meta_ok = (c >= 2 * WINDOW) & (c < 2 * WINDOW + N_META)
    return prev_ok | cur_ok | meta_ok


def _attn_prompt_body(sink_ref, q_ref, k_ref, v_ref, c_ref, s_ref, o_ref, kr_ref):
    seq = q_ref.shape[0]
    nb = (seq - N_META) // WINDOW
    kr_ref[...] = _rope(k_ref[...], c_ref[...], s_ref[...])
    zpad = jnp.zeros((WINDOW - N_META, LANES), F32)
    kmeta = jnp.concatenate([kr_ref[0:N_META, :], zpad], axis=0)
    vmeta = jnp.concatenate([v_ref[0:N_META, :], zpad], axis=0)
    kmeta2 = [_dup_half(kmeta, g) for g in range(N_KV)]
    vmeta2 = [_dup_half(vmeta, g) for g in range(N_KV)]

    qi = lax.broadcasted_iota(jnp.int32, (N_META, 3 * WINDOW), 0)
    c = lax.broadcasted_iota(jnp.int32, (N_META, 3 * WINDOW), 1)
    valid_meta = (c >= 2 * WINDOW) & (c - 2 * WINDOW <= qi)
    kcat = [jnp.concatenate([kmeta2[g], kmeta2[g], kmeta2[g]], axis=0) for g in range(N_KV)]
    vcat = [jnp.concatenate([vmeta2[g], vmeta2[g], vmeta2[g]], axis=0) for g in range(N_KV)]

    def write_meta(p, val):
        o_ref[0:N_META, LANES * p:LANES * (p + 1)] = val

    _attn_frame(q_ref[0:N_META, :], c_ref[0:N_META, :], s_ref[0:N_META, :], kcat, vcat, valid_meta,
                sink_ref, write_meta)

    def frame(n, carry):
        r0 = pl.multiple_of(N_META + WINDOW * n, SUBLANES)
        p0 = pl.multiple_of(jnp.maximum(r0 - WINDOW, 0), SUBLANES)
        kprev, kcur = kr_ref[pl.ds(p0, WINDOW), :], kr_ref[pl.ds(r0, WINDOW), :]
        vprev, vcur = v_ref[pl.ds(p0, WINDOW), :], v_ref[pl.ds(r0, WINDOW), :]
        kc = [jnp.concatenate([_dup_half(kprev, g), _dup_half(kcur, g), kmeta2[g]], axis=0) for g in range(N_KV)]
        vc = [jnp.concatenate([_dup_half(vprev, g), _dup_half(vcur, g), vmeta2[g]], axis=0) for g in range(N_KV)]
        valid = _frame_masks(WINDOW, jnp.where(n == 0, 1, 0))

        def write(p, val):
            o_ref[pl.ds(r0, WINDOW), LANES * p:LANES * (p + 1)] = val

        _attn_frame(q_ref[pl.ds(r0, WINDOW), :], c_ref[pl.ds(r0, WINDOW), :], s_ref[pl.ds(r0, WINDOW), :],
                    kc, vc, valid, sink_ref, write)
        return carry

    lax.fori_loop(0, nb, frame, 0)


def attn_prompt(z, sinks, cos_t, sin_t, n_batch, seq):
    grid_spec = pltpu.PrefetchScalarGridSpec(
        num_scalar_prefetch=0,
        grid=(n_batch,),
        in_specs=[pl.BlockSpec(memory_space=pltpu.SMEM),
                  pl.BlockSpec((seq, D_MIX), lambda b: (b, Z_Q // D_MIX)),
                  pl.BlockSpec((seq, LANES), lambda b: (b, Z_AK // LANES)),
                  pl.BlockSpec((seq, LANES), lambda b: (b, Z_AV // LANES)),
                  pl.BlockSpec((seq, LANES), lambda b: (0, 0)),
                  pl.BlockSpec((seq, LANES), lambda b: (0, 0))],
        out_specs=[pl.BlockSpec((seq, D_MIX), lambda b: (b, 0)),
                   pl.BlockSpec((seq, LANES), lambda b: (b, 0))],
    )
    return pl.pallas_call(
        _attn_prompt_body,
        grid_spec=grid_spec,
        out_shape=[jax.ShapeDtypeStruct((n_batch * seq, D_MIX), F32),
                   jax.ShapeDtypeStruct((n_batch * seq, LANES), F32)],
        compiler_params=_cparams(("parallel",)),
        name="attn_prompt",
    )(sinks, z, z, z, cos_t, sin_t)


def _attn_sample_body(sink_ref, q_ref, k_ref, v_ref, c_ref, s_ref, mk_ref, mv_ref, wk_ref, wv_ref, o_ref, kr_ref,
                      *, n_seq, t_len):
    kr_ref[...] = _rope(k_ref[...], jnp.concatenate([c_ref[...]] * n_seq, axis=0),
                        jnp.concatenate([s_ref[...]] * n_seq, axis=0))
    zpad_m = jnp.zeros((WINDOW - N_META, LANES), F32)
    zpad_c = jnp.zeros((WINDOW - t_len, LANES), F32)
    valid = _frame_masks(t_len, 0)
    for s in range(n_seq):
        rows = slice(s * t_len, (s + 1) * t_len)
        kmeta = jnp.concatenate([mk_ref[s], zpad_m], axis=0)
        vmeta = jnp.concatenate([mv_ref[s], zpad_m], axis=0)
        kcur = jnp.concatenate([kr_ref[rows, :], zpad_c], axis=0)
        vcur = jnp.concatenate([v_ref[rows, :], zpad_c], axis=0)
        kprev, vprev = wk_ref[s], wv_ref[s]
        kc = [jnp.concatenate([_dup_half(kprev, g), _dup_half(kcur, g), _dup_half(kmeta, g)], axis=0)
              for g in range(N_KV)]
        vc = [jnp.concatenate([_dup_half(vprev, g), _dup_half(vcur, g), _dup_half(vmeta, g)], axis=0)
              for g in range(N_KV)]

        def write(p, val, rows=rows):
            o_ref[rows, LANES * p:LANES * (p + 1)] = val

        _attn_frame(q_ref[rows, :], c_ref[...], s_ref[...], kc, vc, valid, sink_ref, write)


def attn_sample(z, row0, sinks, cos_t, sin_t, meta_k, meta_v, win_k, win_v, n_seq_total, t_len, n_seq=8):
    rows = n_seq * t_len
    blk0 = row0 // rows
    grid_spec = pltpu.PrefetchScalarGridSpec(
        num_scalar_prefetch=0,
        grid=(n_seq_total // n_seq,),
        in_specs=[pl.BlockSpec(memory_space=pltpu.SMEM),
                  pl.BlockSpec((rows, D_MIX), lambda i: (blk0 + i, Z_Q // D_MIX)),
                  pl.BlockSpec((rows, LANES), lambda i: (blk0 + i, Z_AK // LANES)),
                  pl.BlockSpec((rows, LANES), lambda i: (blk0 + i, Z_AV // LANES)),
                  pl.BlockSpec((t_len, LANES), lambda i: (0, 0)),
                  pl.BlockSpec((t_len, LANES), lambda i: (0, 0)),
                  pl.BlockSpec((n_seq, N_META, LANES), lambda i: (i, 0, 0)),
                  pl.BlockSpec((n_seq, N_META, LANES), lambda i: (i, 0, 0)),
                  pl.BlockSpec((n_seq, WINDOW, LANES), lambda i: (i, 0, 0)),
                  pl.BlockSpec((n_seq, WINDOW, LANES), lambda i: (i, 0, 0))],
        out_specs=[pl.BlockSpec((rows, D_MIX), lambda i: (i, 0)),
                   pl.BlockSpec((rows, LANES), lambda i: (i, 0))],
    )
    return pl.pallas_call(
        functools.partial(_attn_sample_body, n_seq=n_seq, t_len=t_len),
        grid_spec=grid_spec,
        out_shape=[jax.ShapeDtypeStruct((n_seq_total * t_len, D_MIX), F32),
                   jax.ShapeDtypeStruct((n_seq_total * t_len, LANES), F32)],
        compiler_params=_cparams(("parallel",)),
        name="attn_sample",
    )(sinks, z, z, z, cos_t, sin_t, meta_k, meta_v, win_k, win_v)


def _gla_body(q_ref, k_ref, v_ref, r_ref, lr_ref, w2_ref, b_ref, norm_ref, h0_ref, o_ref, h_ref, ht_ref):
    seq = q_ref.shape[0]
    cs = GLA_CHUNK
    ht_ref[...] = h0_ref[...].T
    row = lax.broadcasted_iota(jnp.int32, (cs, cs), 0)
    col = lax.broadcasted_iota(jnp.int32, (cs, cs), 1)
    tril = row >= col
    tril_bf = jnp.where(tril, 1.0, 0.0).astype(BF16)

    def chunk(r0, nvalid):
        nload = min(cs, seq)

        def load(ref):
            x = ref[pl.ds(r0, nload), :]
            if nload < cs:
                x = jnp.concatenate([x, jnp.zeros((cs - nload, x.shape[1]), F32)], axis=0)
            return x

        q, k, v, gr, lr = load(q_ref), load(k_ref), load(v_ref), load(r_ref), load(lr_ref)
        pre = _bdot(lr, w2_ref[...]) + b_ref[...]
        la = (jnp.minimum(pre, 0.0) - jnp.log(1.0 + jnp.exp(-jnp.abs(pre)))) / GLA_NORMALIZER
        if nvalid < cs:
            live = lax.broadcasted_iota(jnp.int32, (cs, 1), 0) < nvalid
            la = jnp.where(live, la, 0.0)
            k = jnp.where(live, k, 0.0)
            v = jnp.where(live, v, 0.0)
        la_hi, la_lo = _split2(la)
        bc = (jnp.dot(tril_bf, la_hi, preferred_element_type=F32)
              + jnp.dot(tril_bf, la_lo, preferred_element_type=F32))
        b_last = bc[cs - 1:cs, :]
        q_in = q * jnp.exp(bc) * (GLA_DK ** -0.5)
        k_in = k * jnp.exp(-bc)
        k_out = k * jnp.exp(b_last - bc)
        att = jnp.where(tril, _bdot_nt(q_in, k_in), 0.0)
        ht = ht_ref[...]
        o = _bdot(att, v) + _bdot_nt(q_in, ht)
        ht_ref[...] = ht * jnp.exp(b_last) + _bdot_tn(v, k_out)
        o = o * lax.rsqrt(jnp.mean(o * o, axis=-1, keepdims=True) + EPS) * norm_ref[...]
        o = o * _silu(gr)
        o_ref[pl.ds(r0, nvalid), :] = o[0:nvalid, :]

    if seq < cs:
        chunk(0, seq)
    else:
        lead = seq % cs
        if lead:
            chunk(0, lead)

        def step(c, carry):
            chunk(pl.multiple_of(lead + cs * c, SUBLANES), cs)
            return carry

        lax.fori_loop(0, seq // cs, step, 0)
    h_ref[...] = ht_ref[...].T


def gla(z1, z2, blk0, n_seq, seq, w2p, b3, norm3, layer, h0):
    def zspec(width, col0):
        return pl.BlockSpec((seq, width), lambda s, h: (blk0 + s, col0 // width + h))

    return pl.pallas_call(
        _gla_body,
        grid=(n_seq, GLA_HEADS),
        in_specs=[zspec(GLA_DK, Z_GQ), zspec(GLA_DK, Z_GK), zspec(GLA_DV, Z_GV), zspec(GLA_DV, Z_GR),
                  pl.BlockSpec((seq, LANES), lambda s, h: (blk0 + s, Z_GLR // LANES)),
                  pl.BlockSpec((None, LANES, GLA_DK), lambda s, h: (layer, 0, h)),
                  pl.BlockSpec((None, 1, GLA_DK), lambda s, h: (layer, 0, h)),
                  pl.BlockSpec((None, 1, GLA_DV), lambda s, h: (layer, 0, 0)),
                  pl.BlockSpec((None, None, GLA_DK, GLA_DV), lambda s, h: (s, h, 0, 0))],
        out_specs=[pl.BlockSpec((seq, GLA_DV), lambda s, h: (s, h)),
                   pl.BlockSpec((None, None, GLA_DK, GLA_DV), lambda s, h: (s, h, 0, 0))],
        out_shape=[jax.ShapeDtypeStruct((n_seq * seq, D_MIX), F32),
                   jax.ShapeDtypeStruct((n_seq, GLA_HEADS, GLA_DK, GLA_DV), F32)],
        scratch_shapes=[pltpu.VMEM((GLA_DV, GLA_DK), F32)],
        compiler_params=_cparams(("parallel", "parallel")),
        name="gla",
    )(z1, z1, z1, z1, z2, w2p, b3, norm3, h0)


def _seg_ones(n, seg):
    r = lax.broadcasted_iota(jnp.int32, (n, n), 0) // seg
    c = lax.broadcasted_iota(jnp.int32, (n, n), 1) // seg
    return jnp.where(r == c, 1.0, 0.0).astype(BF16)


def _rw_pre_body(zr_ref, zk_ref, zv_ref, zl_ref, pr_ref, pk_ref, pv_ref, pl_ref,
                 mur_ref, muk_ref, muv_ref, mul_ref, w0_ref, w2_ref, a0_ref, a2_ref, g2_ref,
                 kk_ref, ka_ref, rk_ref,
                 r_out, w_out, k_out, v_out, kk_out, b_out, g_out, bonus_out):
    rows = zr_ref.shape[0]
    first = lax.broadcasted_iota(jnp.int32, (rows, 1), 0) == 0

    def shifted(z_ref, p_ref, mu_ref):
        z = z_ref[...]
        prev = jnp.where(first, p_ref[...], pltpu.roll(z, 1, 0))
        return z + (prev - z) * mu_ref[...]

    r = shifted(zr_ref, pr_ref, mur_ref)
    k = shifted(zk_ref, pk_ref, muk_ref)
    v = shifted(zv_ref, pv_ref, muv_ref)
    lo = shifted(zl_ref, pl_ref, mul_ref)
    zw, za, zg = lo[:, 0:LANES], lo[:, LANES:2 * LANES], lo[:, 2 * LANES:4 * LANES]
    w = -_softplus(-(w0_ref[...] + _bdot(jnp.tanh(zw), w2_ref[...]))) - 0.5
    decay = jnp.exp(-jnp.exp(w))
    a = _sigmoid(a0_ref[...] + _bdot(za, a2_ref[...]))
    g = _bdot(_sigmoid(zg), g2_ref[...])
    ones = _seg_ones(D_MIX, RW_HEAD)
    kkr = k * kk_ref[...]
    nrm = jnp.sqrt(_dot2(kkr * kkr, ones))
    kk = kkr / jnp.maximum(nrm, 1e-12)
    k2 = k * (1.0 + (a - 1.0) * ka_ref[...])
    r_out[...] = r
    w_out[...] = decay
    k_out[...] = k2
    v_out[...] = v
    kk_out[...] = kk
    b_out[...] = kk * a
    g_out[...] = g
    bonus_out[...] = _dot2(r * k2 * rk_ref[...], ones) * v


def rw_pre(z, prev, n_seq, tiles_per_seq, tt, blk0, lp, layer):
    nt = tiles_per_seq

    def zspec(width, col0):
        return pl.BlockSpec((tt, width), lambda s, j: (blk0 + s * nt + j, col0 // width))

    def pspec(width):
        return pl.BlockSpec((None, 1, width), lambda s, j: (s * nt + j, 0, 0))

    def lspec(width):
        return pl.BlockSpec((None, 1, width), lambda s, j: (layer, 0, 0))

    def wspec(rows_, cols_):
        return pl.BlockSpec((None, rows_, cols_), lambda s, j: (layer, 0, 0))

    tm_spec = pl.BlockSpec((tt, D_MIX), lambda s, j: (j, s))
    rm_spec = pl.BlockSpec((tt, D_MIX), lambda s, j: (s * nt + j, 0))
    tm_shape = jax.ShapeDtypeStruct((nt * tt, n_seq * D_MIX), F32)
    rm_shape = jax.ShapeDtypeStruct((n_seq * nt * tt, D_MIX), F32)
    return pl.pallas_call(
        _rw_pre_body,
        grid=(n_seq, nt),
        in_specs=[zspec(D_MIX, Z_RR), zspec(D_MIX, Z_RK), zspec(D_MIX, Z_RV), zspec(LORA_W, Z_LORA),
                  pspec(D_MIX), pspec(D_MIX), pspec(D_MIX), pspec(LORA_W),
                  lspec(D_MIX), lspec(D_MIX), lspec(D_MIX), lspec(LORA_W),
                  lspec(D_MIX), wspec(LANES, D_MIX), lspec(D_MIX), wspec(LANES, D_MIX), wspec(2 * LANES, D_MIX),
                  lspec(D_MIX), lspec(D_MIX), lspec(D_MIX)],
        out_specs=[tm_spec] * 6 + [rm_spec] * 2,
        out_shape=[tm_shape] * 6 + [rm_shape] * 2,
        compiler_params=_cparams(("parallel", "parallel")),
        name="rw_pre",
    )(z, z, z, z, prev["r"], prev["k"], prev["v"], prev["l"],
      lp["mu_r"], lp["mu_k"], lp["mu_v"], lp["mu_l"], lp["w0"], lp["w2"], lp["a0"], lp["a2"], lp["g2"],
      lp["kk"], lp["ka"], lp["rk"])


RW_GB = 4
RW_TILE = 256
RW_GROUPS = 4


def _rw_scan_body(r_ref, w_ref, k_ref, v_ref, kk_ref, b_ref, s0_ref, y_ref, s_out_ref, s_ref):
    tb = r_ref.shape[0]
    nj = D_MIX // RW_TILE
    hpt = RW_TILE // RW_HEAD

    @pl.when(pl.program_id(1) == 0)
    def _():
        s_ref[...] = s0_ref[...]

    ones4 = _seg_ones(RW_TILE, RW_HEAD)
    rr = lax.broadcasted_iota(jnp.int32, (RW_HEAD, RW_TILE), 0)
    cc = lax.broadcasted_iota(jnp.int32, (RW_HEAD, RW_TILE), 1)
    eye_rep = jnp.where(cc % RW_HEAD == rr, 1.0, 0.0).astype(BF16)
    hr = lax.broadcasted_iota(jnp.int32, (SUBLANES, RW_TILE), 0)
    hc = lax.broadcasted_iota(jnp.int32, (SUBLANES, RW_TILE), 1)
    head_rows = jnp.where(hc // RW_HEAD == hr, 1.0, 0.0)
    tiles = [(bi, j) for bi in range(RW_GB) for j in range(nj)]

    def step(t, carry):
        def row(ref, bi, j):
            return ref[pl.ds(t, 1), D_MIX * bi + RW_TILE * j:D_MIX * bi + RW_TILE * (j + 1)]

        half = len(tiles) // RW_GROUPS
        groups = tuple(tiles[half * gi:half * (gi + 1)] for gi in range(RW_GROUPS))
        s_olds, boths = [], []
        for grp in groups:
            p_list, vd_list, s_old = [], [], []
            for bi, j in grp:
                s = s_ref[bi, :, RW_TILE * j:RW_TILE * (j + 1)]
                s_old.append(s)
                p_list.append((s * row(kk_ref, bi, j)).astype(BF16))
                vd_list.append(eye_rep * row(v_ref, bi, j).astype(BF16))
            s_olds.append(s_old)
            boths.append(jnp.dot(jnp.concatenate(p_list + vd_list, axis=0), ones4, preferred_element_type=F32))
        n = half * RW_HEAD
        for grp, s_old, both in zip(groups, s_olds, boths):
            for idx, (bi, j) in enumerate(grp):
                lanes = slice(RW_TILE * j, RW_TILE * (j + 1))
                sa = both[RW_HEAD * idx:RW_HEAD * (idx + 1)]
                vcol = both[n + RW_HEAD * idx:n + RW_HEAD * (idx + 1)]
                s_new = s_old[idx] * row(w_ref, bi, j) - sa * row(b_ref, bi, j) + vcol * row(k_ref, bi, j)
                s_ref[bi, :, lanes] = s_new
                y8 = lax.dot_general((head_rows * row(r_ref, bi, j)).astype(BF16), s_new.astype(BF16),
                                     (((1,), (1,)), ((), ())), preferred_element_type=F32)
                y_ref[t, RW_HEADS * bi + hpt * j:RW_HEADS * bi + hpt * (j + 1), :] = y8[0:hpt, :]
        return carry

    lax.fori_loop(0, tb, step, 0, unroll=4)

    @pl.when(pl.program_id(1) == pl.num_programs(1) - 1)
    def _():
        s_out_ref[...] = s_ref[...]


def rw_scan(seqs, s0, tb):
    steps, width = seqs[0].shape
    n_seq = width // D_MIX
    gw = RW_GB * D_MIX
    in_spec = pl.BlockSpec((tb, gw), lambda g, i: (i, g))
    st_spec = pl.BlockSpec((RW_GB, RW_HEAD, D_MIX), lambda g, i: (g, 0, 0))
    return pl.pallas_call(
        _rw_scan_body,
        grid=(n_seq // RW_GB, steps // tb),
        in_specs=[in_spec] * 6 + [st_spec],
        out_specs=[pl.BlockSpec((tb, RW_GB * RW_HEADS, RW_HEAD), lambda g, i: (i, g, 0)), st_spec],
        out_shape=[jax.ShapeDtypeStruct((steps, n_seq * RW_HEADS, RW_HEAD), F32),
                   jax.ShapeDtypeStruct((n_seq, RW_HEAD, D_MIX), F32)],
        scratch_shapes=[pltpu.VMEM((RW_GB, RW_HEAD, D_MIX), F32)],
        compiler_params=_cparams(("parallel", "arbitrary")),
        name="rw_scan",
    )(*seqs, s0)


def _rw_post_body(y_ref, g_ref, bonus_ref, lng_ref, lnb_ref, o_ref):
    y = y_ref[...]
    ones = _seg_ones(D_MIX, RW_HEAD)
    mu = _dot2(y, ones) * (1.0 / RW_HEAD)
    d = y - mu
    var = _dot2(d * d, ones) * (1.0 / RW_HEAD)
    yn = d * lax.rsqrt(var + RW_LN_EPS) * lng_ref[...] + lnb_ref[...]
    o_ref[...] = (yn + bonus_ref[...]) * g_ref[...]


def rw_post(y_tm, g, bonus, n_seq, tiles_per_seq, tt, lng3, lnb3, layer):
    nt = tiles_per_seq
    rm_spec = pl.BlockSpec((tt, D_MIX), lambda s, j: (s * nt + j, 0))
    lspec = pl.BlockSpec((None, 1, D_MIX), lambda s, j: (layer, 0, 0))
    return pl.pallas_call(
        _rw_post_body,
        grid=(n_seq, nt),
        in_specs=[pl.BlockSpec((tt, D_MIX), lambda s, j: (j, s)), rm_spec, rm_spec, lspec, lspec],
        out_specs=rm_spec,
        out_shape=jax.ShapeDtypeStruct((n_seq * nt * tt, D_MIX), F32),
        compiler_params=_cparams(("parallel", "parallel")),
        name="rw_post",
    )(y_tm, g, bonus, lng3, lnb3)


S5_W = S5_GROUPS * S5_STATE
S5_KT = 256
S5_NT = S5_KT // S5_GROUP * S5_STATE


def _s5_body(u_ref, bbr_ref, bbi_ref, ccr_ref, cci_ref, d_ref, a_ref, apow_ref, gw_ref, gb_ref, h0r_ref, h0i_ref,
             o_ref, hr_out, hi_out, xr_ref, xi_ref, cr_ref, ci_ref):
    tt = u_ref.shape[0]
    nkt = D_MIX // S5_KT

    @pl.when(pl.program_id(1) == 0)
    def _():
        cr_ref[...] = h0r_ref[...]
        ci_ref[...] = h0i_ref[...]

    u = u_ref[...]
    for jt in range(nkt):
        ub = u[:, S5_KT * jt:S5_KT * (jt + 1)]
        xr_ref[:, S5_NT * jt:S5_NT * (jt + 1)] = _bdot(ub, bbr_ref[jt])
        xi_ref[:, S5_NT * jt:S5_NT * (jt + 1)] = _bdot(ub, bbi_ref[jt])

    sub = lax.broadcasted_iota(jnp.int32, (SUBLANES, S5_W), 0)
    a_re = [a_ref[2 * i:2 * i + 1, :] for i in range(3)]
    a_im = [a_ref[2 * i + 1:2 * i + 2, :] for i in range(3)]
    pw_re, pw_im = apow_ref[0], apow_ref[1]

    def group(gi, carry):
        r0 = pl.multiple_of(gi * SUBLANES, SUBLANES)
        hr, hi = xr_ref[pl.ds(r0, SUBLANES), :], xi_ref[pl.ds(r0, SUBLANES), :]
        for i, sh in enumerate((1, 2, 4)):
            keep = sub >= sh
            sr = jnp.where(keep, pltpu.roll(hr, sh, 0), 0.0)
            si = jnp.where(keep, pltpu.roll(hi, sh, 0), 0.0)
            hr, hi = hr + a_re[i] * sr - a_im[i] * si, hi + a_re[i] * si + a_im[i] * sr
        c_r, c_i = cr_ref[...], ci_ref[...]
        hr = hr + pw_re * c_r - pw_im * c_i
        hi = hi + pw_re * c_i + pw_im * c_r
        xr_ref[pl.ds(r0, SUBLANES), :] = hr
        xi_ref[pl.ds(r0, SUBLANES), :] = hi
        cr_ref[...] = hr[SUBLANES - 1:SUBLANES, :]
        ci_ref[...] = hi[SUBLANES - 1:SUBLANES, :]
        return carry

    lax.fori_loop(0, tt // SUBLANES, group, 0)

    ys = []
    for jt in range(nkt):
        hr = xr_ref[:, S5_NT * jt:S5_NT * (jt + 1)]
        hi = xi_ref[:, S5_NT * jt:S5_NT * (jt + 1)]
        ys.append(_bdot(hr, ccr_ref[jt]) - _bdot(hi, cci_ref[jt]))
    y = jnp.concatenate(ys, axis=1) + d_ref[...] * u
    ya = 0.5 * y * (1.0 + jnp.tanh(math.sqrt(2.0 / math.pi) * (y + 0.044715 * (y * y * y))))
    o_ref[...] = ya * _sigmoid(_bdot(ya, gw_ref[...]) + gb_ref[...])

    @pl.when(pl.program_id(1) == pl.num_programs(1) - 1)
    def _():
        hr_out[...] = cr_ref[...]
        hi_out[...] = ci_ref[...]


def s5(z, blk0, n_seq, tiles_per_seq, tt, sp, layer, h0r, h0i):
    nt = tiles_per_seq
    nkt = D_MIX // S5_KT

    def const3(shape):
        return pl.BlockSpec(shape, lambda s, j: (0,) * len(shape))

    st_spec = pl.BlockSpec((None, 1, S5_W), lambda s, j: (s, 0, 0))
    return pl.pallas_call(
        _s5_body,
        grid=(n_seq, nt),
        in_specs=[pl.BlockSpec((tt, D_MIX), lambda s, j: (blk0 + s * nt + j, Z_S5 // D_MIX)),
                  const3((nkt, S5_KT, S5_NT)), const3((nkt, S5_KT, S5_NT)),
                  const3((nkt, S5_NT, S5_KT)), const3((nkt, S5_NT, S5_KT)),
                  const3((1, D_MIX)), const3((6, S5_W)), const3((2, SUBLANES, S5_W)),
                  pl.BlockSpec((None, D_MIX, D_MIX), lambda s, j: (layer, 0, 0)),
                  pl.BlockSpec((None, 1, D_MIX), lambda s, j: (layer, 0, 0)),
                  st_spec, st_spec],
        out_specs=[pl.BlockSpec((tt, D_MIX), lambda s, j: (s * nt + j, 0)), st_spec, st_spec],
        out_shape=[jax.ShapeDtypeStruct((n_seq * nt * tt, D_MIX), F32),
                   jax.ShapeDtypeStruct((n_seq, 1, S5_W), F32),
                   jax.ShapeDtypeStruct((n_seq, 1, S5_W), F32)],
        scratch_shapes=[pltpu.VMEM((tt, S5_W), F32), pltpu.VMEM((tt, S5_W), F32),
                        pltpu.VMEM((1, S5_W), F32), pltpu.VMEM((1, S5_W), F32)],
        compiler_params=_cparams(("parallel", "arbitrary")),
        name="s5",
    )(z, sp["bbr"], sp["bbi"], sp["ccr"], sp["cci"], sp["d"], sp["a"], sp["apow"], sp["glu_w"], sp["glu_b"], h0r, h0i)


def _s5_tables(lam_re, lam_im, log_step, b_re, b_im, c_re, c_im, d):
    dt = jnp.exp(log_step.astype(F32))[:, None]
    lr = jnp.minimum(lam_re.astype(F32), -1e-4)
    li = lam_im.astype(F32)
    mag = jnp.exp(lr * dt)
    ab_re, ab_im = mag * jnp.cos(li * dt), mag * jnp.sin(li * dt)
    den = lr * lr + li * li
    f_re = ((ab_re - 1.0) * lr + ab_im * li) / den
    f_im = (ab_im * lr - (ab_re - 1.0) * li) / den
    bb_re = f_re[..., None] * b_re - f_im[..., None] * b_im
    bb_im = f_re[..., None] * b_im + f_im[..., None] * b_re
    gpt = S5_KT // S5_GROUP
    nkt = S5_GROUPS // gpt
    eye = jnp.eye(gpt, dtype=F32)

    def in_tiles(bb):
        t = bb.reshape(nkt, gpt, S5_STATE, S5_GROUP)
        t = jnp.einsum('jgph,gk->jghkp', t, eye)
        return t.reshape(nkt, S5_KT, S5_NT)

    def out_tiles(cc):
        t = cc.reshape(nkt, gpt, S5_GROUP, S5_STATE)
        t = jnp.einsum('jghp,gk->jgpkh', t, eye)
        return t.reshape(nkt, S5_NT, S5_KT)

    ar, ai = ab_re.reshape(1, S5_W), ab_im.reshape(1, S5_W)
    pows_r, pows_i = [ar], [ai]
    for _ in range(SUBLANES - 1):
        pr, pi = pows_r[-1], pows_i[-1]
        pows_r.append(pr * ar - pi * ai)
        pows_i.append(pr * ai + pi * ar)
    a_tab = jnp.concatenate([pows_r[0], pows_i[0], pows_r[1], pows_i[1], pows_r[3], pows_i[3]], axis=0)
    apow = jnp.stack([jnp.concatenate(pows_r, axis=0), jnp.concatenate(pows_i, axis=0)])
    return {"bbr": in_tiles(bb_re).astype(BF16), "bbi": in_tiles(bb_im).astype(BF16),
            "ccr": out_tiles(c_re.astype(F32)).astype(BF16), "cci": out_tiles(c_im.astype(F32)).astype(BF16), "d": d.reshape(1, D_MIX).astype(F32), "a": a_tab, "apow": apow}


def _merge_body(o0, o1, o2, o3, wb_ref, g0, g1, g2, g3, out_ref):
    acc = None
    for i, (o_ref, g_ref) in enumerate(((o0, g0), (o1, g1), (o2, g2), (o3, g3))):
        term = _sigmoid(g_ref[...]) * _bdot(o_ref[...], wb_ref[i])
        acc = term if acc is None else acc + term
    out_ref[...] = acc.astype(out_ref.dtype)


def merge(outs, w_branch, zgate, layer, *, tm, tn):
    m = outs[0].shape[0]
    nj = D_MODEL // tn
    o_spec = pl.BlockSpec((tm, D_MIX), lambda i, j: (i, 0))

    def gspec(b):
        return pl.BlockSpec((tm, tn), lambda i, j: (i, b * nj + j))

    return pl.pallas_call(
        _merge_body,
        grid=(m // tm, nj),
        in_specs=[o_spec] * 4 + [pl.BlockSpec((None, N_BRANCH, D_MIX, tn), lambda i, j: (layer, 0, 0, j))]
        + [gspec(b) for b in range(N_BRANCH)],
        out_specs=pl.BlockSpec((tm, tn), lambda i, j: (i, j)),
        out_shape=jax.ShapeDtypeStruct((m, D_MODEL), BF16),
        compiler_params=_cparams(("parallel", "arbitrary")),
        name="merge",
    )(*outs, w_branch, zgate, zgate, zgate, zgate)


def _router_body(x_ref, g_ref, wr_ref, br_ref, xn_ref, route_ref):
    x = x_ref[...]
    xn = x * lax.rsqrt(jnp.mean(x * x, axis=-1, keepdims=True) + EPS) * g_ref[...]
    xn_ref[...] = xn
    logits = _bdot(xn, wr_ref[...]) + br_ref[...]
    lane = lax.broadcasted_iota(jnp.int32, logits.shape, 1)
    big = jnp.int32(1 << 20)
    ninf = -jnp.inf
    lg = jnp.where(lane < N_GROUPS, logits, ninf)
    gmax = jnp.max(lg, axis=-1, keepdims=True)
    g_top = jnp.min(jnp.where(lg == gmax, lane, big), axis=-1, keepdims=True)
    pg = 1.0 / jnp.sum(jnp.exp(lg - gmax), axis=-1, keepdims=True)
    e_lane = lane - N_GROUPS
    in_group = (e_lane >= 0) & (e_lane < N_EXPERTS) & ((e_lane // EXP_PER_GROUP) == g_top)
    le = jnp.where(in_group, logits, ninf)
    m1 = jnp.max(le, axis=-1, keepdims=True)
    i1 = jnp.min(jnp.where(le == m1, lane, big), axis=-1, keepdims=True)
    le2 = jnp.where(lane == i1, ninf, le)
    m2 = jnp.max(le2, axis=-1, keepdims=True)
    i2 = jnp.min(jnp.where(le2 == m2, lane, big), axis=-1, keepdims=True)
    e2 = jnp.exp(m2 - m1)
    w1 = pg / (1.0 + e2)
    w2 = pg * e2 / (1.0 + e2)
    out = jnp.where(lane == 0, (i1 - N_GROUPS).astype(F32),
                    jnp.where(lane == 1, (i2 - N_GROUPS).astype(F32),
                              jnp.where(lane == 2, w1, jnp.where(lane == 3, w2, 0.0))))
    route_ref[...] = out


def router(x, g3, wr3, br3, layer, tm):
    m, d = x.shape
    return pl.pallas_call(
        _router_body,
        grid=(m // tm,),
        in_specs=[pl.BlockSpec((tm, d), lambda i: (i, 0)),
                  pl.BlockSpec((None, 1, d), lambda i: (layer, 0, 0)),
                  pl.BlockSpec((None, d, LANES), lambda i: (layer, 0, 0)),
                  pl.BlockSpec((None, 1, LANES), lambda i: (layer, 0, 0))],
        out_specs=[pl.BlockSpec((tm, d), lambda i: (i, 0)), pl.BlockSpec((tm, LANES), lambda i: (i, 0))],
        out_shape=[jax.ShapeDtypeStruct((m, d), F32), jax.ShapeDtypeStruct((m, LANES), F32)],
        compiler_params=_cparams(("parallel",)),
        name="moe_router",
    )(x, g3, wr3, br3)


def _gather_rows(src_hbm, idx_ref, base, dst, sem, n):
    for r in range(n):
        pltpu.make_async_copy(src_hbm.at[pl.ds(idx_ref[base + r], 1), :], dst.at[pl.ds(r, 1), :], sem).start()


def _wait_rows(src_hbm, dst, sem, n):
    for r in range(n):
        pltpu.make_async_copy(src_hbm.at[pl.ds(0, 1), :], dst.at[pl.ds(r, 1), :], sem).wait()


MOE_MB = 256


def _moe_up_body(exp_ref, tok_ref, nused_ref, x_hbm, w1_ref, w3_ref, h_ref, xbuf, sems):
    i = pl.program_id(0)
    n_used = nused_ref[0]
    slot = i % 2

    @pl.when(i == 0)
    def _():
        _gather_rows(x_hbm, tok_ref, 0, xbuf.at[0], sems.at[0], MOE_MB)

    @pl.when(i + 1 < n_used)
    def _():
        _gather_rows(x_hbm, tok_ref, (i + 1) * MOE_MB, xbuf.at[1 - slot], sems.at[1 - slot], MOE_MB)

    @pl.when(i < n_used)
    def _():
        _wait_rows(x_hbm, xbuf.at[slot], sems.at[slot], MOE_MB)
        xb = xbuf[slot].astype(BF16)
        a = jnp.dot(xb, w1_ref[...].astype(BF16), preferred_element_type=F32)
        b = jnp.dot(xb, w3_ref[...].astype(BF16), preferred_element_type=F32)
        h_ref[...] = _silu(a) * b

    @pl.when(i >= n_used)
    def _():
        h_ref[...] = jnp.zeros(h_ref.shape, F32)


def moe_up(blk_exp, slot_tok, n_used, xn, w1, w3, layer):
    n_blocks = blk_exp.shape[0]
    d = xn.shape[1]
    grid_spec = pltpu.PrefetchScalarGridSpec(
        num_scalar_prefetch=3,
        grid=(n_blocks,),
        in_specs=[pl.BlockSpec(memory_space=pl.ANY),
                  pl.BlockSpec((None, None, d, D_EXPERT), lambda i, e, t, u: (layer, e[i], 0, 0)),
                  pl.BlockSpec((None, None, d, D_EXPERT), lambda i, e, t, u: (layer, e[i], 0, 0))],
        out_specs=pl.BlockSpec((MOE_MB, D_EXPERT), lambda i, e, t, u: (i, 0)),
        scratch_shapes=[pltpu.VMEM((2, MOE_MB, d), F32), pltpu.SemaphoreType.DMA((2,))],
    )
    return pl.pallas_call(
        _moe_up_body,
        grid_spec=grid_spec,
        out_shape=jax.ShapeDtypeStruct((n_blocks * MOE_MB, D_EXPERT), F32),
        compiler_params=_cparams(("arbitrary",)),
        name="moe_up",
    )(blk_exp, slot_tok, n_used, xn, w1, w3)


def _moe_down_body(exp_ref, nused_ref, h_ref, w2_ref, sw_ref, y_ref):
    i = pl.program_id(0)

    @pl.when(i < nused_ref[0])
    def _():
        y_ref[...] = (jnp.dot(h_ref[...].astype(BF16), w2_ref[...].astype(BF16), preferred_element_type=F32)
                      * sw_ref[...])

    @pl.when(i >= nused_ref[0])
    def _():
        y_ref[...] = jnp.zeros(y_ref.shape, F32)


def moe_down(blk_exp, n_used, h, w2, slot_w, layer):
    n_blocks = blk_exp.shape[0]
    d = w2.shape[3]
    grid_spec = pltpu.PrefetchScalarGridSpec(
        num_scalar_prefetch=2,
        grid=(n_blocks,),
        in_specs=[pl.BlockSpec((MOE_MB, D_EXPERT), lambda i, e, u: (i, 0)),
                  pl.BlockSpec((None, None, D_EXPERT, d), lambda i, e, u: (layer, e[i], 0, 0)),
                  pl.BlockSpec((MOE_MB, 1), lambda i, e, u: (i, 0))],
        out_specs=pl.BlockSpec((MOE_MB, d), lambda i, e, u: (i, 0)),
    )
    return pl.pallas_call(
        _moe_down_body,
        grid_spec=grid_spec,
        out_shape=jax.ShapeDtypeStruct((n_blocks * MOE_MB, d), F32),
        compiler_params=_cparams(("arbitrary",)),
        name="moe_down",
    )(blk_exp, n_used, h, w2, slot_w)


MOE_TC = 64


def _moe_combine_body(slot_ref, x_ref, y_hbm, o_ref, ybuf, sems):
    i = pl.program_id(0)
    nstep = pl.num_programs(0)
    slot = i % 2
    n = TOP_K * MOE_TC

    @pl.when(i == 0)
    def _():
        _gather_rows(y_hbm, slot_ref, 0, ybuf.at[0], sems.at[0], n)

    @pl.when(i + 1 < nstep)
    def _():
        _gather_rows(y_hbm, slot_ref, (i + 1) * n, ybuf.at[1 - slot], sems.at[1 - slot], n)

    _wait_rows(y_hbm, ybuf.at[slot], sems.at[slot], n)
    o_ref[...] = x_ref[...] + (ybuf[slot, 0:MOE_TC, :] + ybuf[slot, MOE_TC:n, :])


def moe_combine(slot_of, x, yb):
    m, d = x.shape
    grid_spec = pltpu.PrefetchScalarGridSpec(
        num_scalar_prefetch=1,
        grid=(m // MOE_TC,),
        in_specs=[pl.BlockSpec((MOE_TC, d), lambda i, s: (i, 0)),
                  pl.BlockSpec(memory_space=pl.ANY)],
        out_specs=pl.BlockSpec((MOE_TC, d), lambda i, s: (i, 0)),
        scratch_shapes=[pltpu.VMEM((2, TOP_K * MOE_TC, d), F32), pltpu.SemaphoreType.DMA((2,))],
    )
    return pl.pallas_call(
        _moe_combine_body,
        grid_spec=grid_spec,
        out_shape=jax.ShapeDtypeStruct((m, d), F32),
        compiler_params=_cparams(("arbitrary",)),
        name="moe_combine",
    )(slot_of, x, yb)


def _moe_plan(route, n_tok):
    e_idx = route[:, 0:TOP_K].astype(jnp.int32)
    w = route[:, TOP_K:2 * TOP_K]
    n_assign = n_tok * TOP_K
    e_flat = e_idx.reshape(n_assign)
    onehot = (e_flat[:, None] == jnp.arange(N_EXPERTS, dtype=jnp.int32)[None, :]).astype(jnp.int32)
    csum = jnp.cumsum(onehot, axis=0)
    counts = csum[-1]
    rank = jnp.sum((csum - onehot) * onehot, axis=1)
    padded = (counts + MOE_MB - 1) // MOE_MB * MOE_MB
    pad_end = jnp.cumsum(padded)
    pad_start = pad_end - padded
    dest = pad_start[e_flat] + rank
    n_blocks = -(-(n_assign + N_EXPERTS * (MOE_MB - 1)) // MOE_MB)
    n_slots = n_blocks * MOE_MB
    tok_flat = jnp.repeat(jnp.arange(n_tok, dtype=jnp.int32), TOP_K)
    packed = jnp.stack([tok_flat.astype(F32), w.reshape(n_assign)], axis=1)
    slots = jnp.zeros((n_slots, 2), F32).at[dest].set(packed)
    slot_tok = slots[:, 0].astype(jnp.int32)
    slot_w = slots[:, 1:2]
    n_used = (pad_end[N_EXPERTS - 1] // MOE_MB).astype(jnp.int32)
    blk = jnp.minimum(jnp.arange(n_blocks, dtype=jnp.int32), n_used - 1)
    blk_exp = jnp.sum((pad_end[None, :] <= (blk * MOE_MB)[:, None]).astype(jnp.int32), axis=1).astype(jnp.int32)
    d2 = dest.reshape(n_tok // MOE_TC, MOE_TC, TOP_K)
    slot_of = jnp.transpose(d2, (0, 2, 1)).reshape(-1)
    return blk_exp, slot_tok, slot_w, slot_of, n_used.reshape(1)


def kernel(x_prompt, x_sample, cache_attn_meta_k, cache_attn_meta_v, cache_attn_win_k, cache_attn_win_v, state_gla, state_rwkv, state_rwkv_shift, state_s5_re, state_s5_im, meta_tokens, norm_mix, w_in, w_gla_gate, b_gla_gate, gla_norm, attn_sinks, rw_mu, rw_w0, rw_w2, rw_a0, rw_a2, rw_g2, rw_kk, rw_ka, rw_rk, rw_ln_g, rw_ln_b, s5_lam_re, s5_lam_im, s5_log_step, s5_b_re, s5_b_im, s5_c_re, s5_c_im, s5_d, s5_glu_w, s5_glu_b, w_branch, w_out, norm_moe, w_router_group, b_router_group, w_router_expert, b_router_expert, w_exp_gate, w_exp_up, w_exp_down, norm_final):
    depth = w_in.shape[0]
    nb, seq_p = x_prompt.shape[0], x_prompt.shape[1]
    db, t_len = x_sample.shape[0], x_sample.shape[1]
    seq = seq_p + N_META
    rows_p = nb * seq
    rows_s = db * t_len
    n_tok = rows_p + rows_s
    wb_len = cache_attn_win_k.shape[2]

    tm_big = n_tok // 7
    tm_mid = n_tok // 14
    tm_small = n_tok // 19
    rw_nt = 6
    rw_tt = seq // rw_nt
    rw_tb = seq // 43
    s5_tt = seq // 6

    pieces = []
    for bq in range(nb):
        pieces += [meta_tokens.astype(F32), x_prompt[bq]]
    x = jnp.concatenate(pieces + [x_sample.reshape(rows_s, D_MODEL)], axis=0)

    cos_p, sin_p = _rope_tables(jnp.arange(seq, dtype=jnp.int32))
    cos_s, sin_s = _rope_tables(PAST_LEN + jnp.arange(t_len, dtype=jnp.int32))

    def r3(a):
        return a.reshape(depth, 1, a.shape[-1]).astype(F32)

    norm_mix3, norm_moe3 = r3(norm_mix), r3(norm_moe)
    b_gla3, gla_norm3 = r3(b_gla_gate), r3(gla_norm)
    w2p = jnp.pad(w_gla_gate, ((0, 0), (0, LANES - GLA_RANK), (0, 0)))

    def lora_cols(a):
        c = 3 * D_MIX
        zw, za, zg = a[..., c:c + 64], a[..., c + 64:c + 128], a[..., c + 128:c + 288]
        pad = lambda t, n: jnp.pad(t, [(0, 0)] * (t.ndim - 1) + [(0, n - t.shape[-1])])
        lo = jnp.concatenate([pad(zw, LANES), pad(za, LANES), pad(zg, 2 * LANES)], axis=-1)
        return a[..., 0:D_MIX], a[..., D_MIX:2 * D_MIX], a[..., 2 * D_MIX:c], lo

    mu_r, mu_k, mu_v, mu_l = lora_cols(rw_mu)
    rw_params = {
        "mu_r": r3(mu_r), "mu_k": r3(mu_k), "mu_v": r3(mu_v), "mu_l": r3(mu_l),
        "w0": r3(rw_w0), "a0": r3(rw_a0), "kk": r3(rw_kk), "ka": r3(rw_ka),
        "rk": r3(rw_rk.reshape(depth, D_MIX)),
        "w2": jnp.pad(rw_w2, ((0, 0), (0, LANES - RW_DECAY_RANK), (0, 0))),
        "a2": jnp.pad(rw_a2, ((0, 0), (0, LANES - RW_A_RANK), (0, 0))),
        "g2": jnp.pad(rw_g2, ((0, 0), (0, 2 * LANES - RW_G_RANK), (0, 0))),
    }
    ln_g3, ln_b3 = r3(rw_ln_g), r3(rw_ln_b)
    glu_b3 = r3(s5_glu_b)
    wr3 = jnp.pad(jnp.concatenate([w_router_group, w_router_expert], axis=2).astype(F32),
                  ((0, 0), (0, 0), (0, LANES - N_GROUPS - N_EXPERTS)))
    br3 = jnp.pad(jnp.concatenate([b_router_group, b_router_expert], axis=1).astype(F32),
                  ((0, 0), (0, LANES - N_GROUPS - N_EXPERTS))).reshape(depth, 1, LANES)

    w_in2d = w_in.reshape(depth * D_MODEL, w_in.shape[2])
    rec = [[] for _ in range(16)]
    zeros_gla =jnp.zeros((nb, GLA_HEADS, GLA_DK, GLA_DV), F32)
    zeros_rw = jnp.zeros((nb, RW_HEAD, D_MIX), F32)
    zeros_s5 = jnp.zeros((nb, 1, S5_W), F32)

    for l in range(depth):
        zcol = lambda n: jnp.zeros((D_MODEL, n), F32)
        wcols = lambda a, n: lax.slice(w_in2d, (l * D_MODEL, a), ((l + 1) * D_MODEL, a + n))
        rw0 = _SRC_RW
        w_mix = jnp.concatenate([
            wcols(_SRC_Q, 1024), wcols(rw0, 3072), wcols(_SRC_S5, 1024), wcols(_SRC_GV, 2048),
            wcols(rw0 + 3072, 64), zcol(64), wcols(rw0 + 3136, 64), zcol(64), wcols(rw0 + 3200, 160), zcol(96),
            wcols(_SRC_GQ, 1024), wcols(_SRC_K, 256),
            wcols(_SRC_GLR, 16), zcol(112), zcol(NZ_MIX - Z_GLR - LANES)], axis=1)[None]
        w_gate = wcols(_SRC_GATE, N_BRANCH * D_MODEL)[None]
        xn = rmsnorm_rows(x, norm_mix3, l, BF16, tm_small)
        z = matmul_rows(xn, w_mix, 0, tm=tm_big, tn=512, name="in_proj_mix")
        z1 = z2 = z
        zgate = matmul_rows(xn, w_gate, 0, tm=tm_big, tn=512, name="in_proj_gate")

        sinks = attn_sinks[l].astype(F32)
        o_att_p, kr_p = attn_prompt(z1, sinks, cos_p, sin_p, nb, seq)
        mk = cache_attn_meta_k[l].reshape(db, N_META, LANES)
        mv = cache_attn_meta_v[l].reshape(db, N_META, LANES)
        wk = cache_attn_win_k[l].reshape(db, wb_len, LANES)
        wv = cache_attn_win_v[l].reshape(db, wb_len, LANES)
        o_att_s, kr_s = attn_sample(z1, rows_p, sinks, cos_s, sin_s, mk, mv, wk, wv, db, t_len)
        def state_rows(a, lo, hi):
            return jnp.stack([a[bq * seq + lo:bq * seq + hi] for bq in range(nb)]).reshape(nb, hi - lo, N_KV, HEAD_DIM)

        v_flat = z1[:, Z_AV:Z_AV + LANES]
        v_s = v_flat[rows_p:].reshape(db, t_len, N_KV, HEAD_DIM)
        k_s = kr_s.reshape(db, t_len, N_KV, HEAD_DIM)
        new_wk = jnp.concatenate([cache_attn_win_k[l].astype(F32), k_s], axis=1)[:, -wb_len:]
        new_wv = jnp.concatenate([cache_attn_win_v[l].astype(F32), v_s], axis=1)[:, -wb_len:]

        o_gla_p, gla_p = gla(z1, z2, 0, nb, seq, w2p, b_gla3, gla_norm3, l, zeros_gla)
        o_gla_s, gla_s = gla(z1, z2, rows_p // t_len, db, t_len, w2p, b_gla3, gla_norm3, l,
                             state_gla[l].astype(F32))

        last_rows = lax.slice(z2, (rw_tt - 1, 0), (rows_p, NZ_MIX), (rw_tt, 1))
        prev_all = jnp.concatenate([jnp.zeros((1, NZ_MIX), F32), last_rows[:-1]], axis=0)
        starts_seq = (jnp.arange(nb * rw_nt, dtype=jnp.int32) % rw_nt == 0)[:, None]
        prev_all = jnp.where(starts_seq, 0.0, prev_all)[:, None, :]

        def prev_cols(col0, width):
            return prev_all[:, :, col0:col0 + width]

        prev_p = {"r": prev_cols(Z_RR, D_MIX), "k": prev_cols(Z_RK, D_MIX), "v": prev_cols(Z_RV, D_MIX),
                  "l": prev_cols(Z_LORA, LORA_W)}
        sr, sk, sv, sl = lora_cols(state_rwkv_shift[l].astype(F32))
        prev_s = {"r": sr[:, None], "k": sk[:, None], "v": sv[:, None], "l": sl[:, None]}

        pre_p = rw_pre(z2, prev_p, nb, rw_nt, rw_tt, 0, rw_params, l)
        pre_s = rw_pre(z2, prev_s, db, 1, t_len, rows_p // t_len, rw_params, l)
        y_p, st_p = rw_scan(pre_p[:6], zeros_rw, rw_tb)
        s0_s = jnp.transpose(state_rwkv[l].astype(F32), (0, 2, 1, 3)).reshape(db, RW_HEAD, D_MIX)
        y_s, st_s = rw_scan(pre_s[:6], s0_s, t_len)
        o_rw_p = rw_post(y_p.reshape(seq, nb * D_MIX), pre_p[6], pre_p[7], nb, rw_nt, rw_tt, ln_g3, ln_b3, l)
        o_rw_s = rw_post(y_s.reshape(t_len, db * D_MIX), pre_s[6], pre_s[7], db, 1, t_len, ln_g3, ln_b3, l)

        def rw_state(st, n):
            return jnp.transpose(st.reshape(n, RW_HEAD, RW_HEADS, RW_HEAD), (0, 2, 1, 3))

        def shift_out(zl):
            lo = zl[:, Z_LORA:Z_LORA + LORA_W]
            return jnp.concatenate([zl[:, Z_RR:Z_RR + 3 * D_MIX], lo[:, 0:64], lo[:, 128:192], lo[:, 256:416]], axis=1)

        sh_p = shift_out(jnp.concatenate([z2[(bq + 1) * seq - 1:(bq + 1) * seq] for bq in range(nb)], axis=0))
        sh_s = shift_out(z2[rows_p:].reshape(db, t_len, NZ_MIX)[:, t_len - 1])

        s5p = _s5_tables(s5_lam_re[l], s5_lam_im[l], s5_log_step[l], s5_b_re[l], s5_b_im[l], s5_c_re[l], s5_c_im[l],
                         s5_d[l])
        s5p["glu_w"] = s5_glu_w
        s5p["glu_b"] = glu_b3
        o_s5_p, re_p, im_p = s5(z2, 0, nb, 6, s5_tt, s5p, l, zeros_s5, zeros_s5)
        o_s5_s, re_s, im_s = s5(z2, rows_p // t_len, db, 1, t_len, s5p, l,
                                state_s5_re[l].astype(F32).reshape(db, 1, S5_W),
                                state_s5_im[l].astype(F32).reshape(db, 1, S5_W))

        outs = [jnp.concatenate([a, b], axis=0).astype(BF16) for a, b in
                ((o_att_p, o_att_s), (o_gla_p, o_gla_s), (o_rw_p, o_rw_s), (o_s5_p, o_s5_s))]
        merged = merge(outs, w_branch, zgate, l, tm=tm_big, tn=256)
        x = matmul_rows(merged, w_out, l, tm=tm_big, tn=512, residual=x, name="out_proj")

        xn2, route = router(x, norm_moe3, wr3, br3, l, tm_small)
        blk_exp, slot_tok, slot_w, slot_of, n_used = _moe_plan(route, n_tok)
        hmid = moe_up(blk_exp, slot_tok, n_used, xn2, w_exp_gate, w_exp_up, l)
        yb = moe_down(blk_exp, n_used, hmid, w_exp_down, slot_w, l)
        x = moe_combine(slot_of, x, yb)

        vals = (state_rows(kr_p, 0, N_META), state_rows(v_flat, 0, N_META),
                state_rows(kr_p, seq - WINDOW, seq), state_rows(v_flat, seq - WINDOW, seq),
                gla_p, rw_state(st_p, nb), sh_p, re_p.reshape(nb, S5_GROUPS, S5_STATE),
                im_p.reshape(nb, S5_GROUPS, S5_STATE),
                new_wk, new_wv, gla_s, rw_state(st_s, db), sh_s,
                re_s.reshape(db, S5_GROUPS, S5_STATE), im_s.reshape(db, S5_GROUPS, S5_STATE))
        for i in range(16):
            rec[i].append(vals[i])

    y = rmsnorm_rows(x, norm_final.reshape(1, 1, D_MODEL).astype(F32), 0, F32, tm_small)
    y_prompt = jnp.stack([y[bq * seq + N_META:(bq + 1) * seq] for bq in range(nb)])
    y_sample = y[rows_p:].reshape(db, t_len, D_MODEL)
    return (y_prompt, y_sample, *[jnp.stack(r) for r in rec])
```

```python
import functools
import math

import numpy as np
import jax
import jax.numpy as jnp
from jax import lax
from jax.experimental import pallas as pl
from jax.experimental.pallas import tpu as pltpu

F32 = jnp.float32
BF16 = jnp.bfloat16

D_MODEL = 4096
N_META = 16
EPS = 1e-5
D_MIX = 1024
HEAD_DIM = 64
N_HEADS = 16
N_KV = 2
WINDOW = 128
ROPE_DIM = 16
ROPE_THETA = 500000.0
NEG_INF = -1e30
GLA_HEADS = 4
GLA_DK = 128
GLA_DV = 256
GLA_RANK = 16
GLA_NORMALIZER = 16.0
GLA_CHUNK = 64
RW_HEAD = 64
RW_HEADS = 16
RW_DECAY_RANK = 64
RW_A_RANK = 64
RW_G_RANK = 160
RW_IN = 3 * D_MIX + RW_DECAY_RANK + RW_A_RANK + RW_G_RANK
RW_LN_EPS = 64e-5
S5_GROUP = 16
S5_GROUPS = 64
S5_STATE = 64
N_GROUPS = 4
EXP_PER_GROUP = 8
N_EXPERTS = 32
TOP_K = 2
D_EXPERT = 512
MOE_BLOCK = 128
PAST_LEN = 16384
N_BRANCH = 4

LANES = 128
SUBLANES = 8
VMEM_BUDGET = 56 * 1024 * 1024

Z_Q = 0
Z_RR = 1024
Z_RK = 2048
Z_RV = 3072
Z_S5 = 4096
Z_GV = 5120
Z_GR = 6144
Z_LORA = 7168
Z_GQ = 7680
Z_GK = 8192
Z_AK = 8704
Z_AV = 8832
Z_GLR = 8960
NZ_MIX = 9216
LORA_W = 512

_SRC_Q, _SRC_K, _SRC_V, _SRC_GQ, _SRC_GK, _SRC_GV, _SRC_GR, _SRC_GLR = 0, 1024, 1152, 1280, 1792, 2304, 3328, 4352
_SRC_RW = 4368
_SRC_S5 = _SRC_RW + RW_IN
_SRC_GATE = _SRC_S5 + D_MIX


def _cparams(sem, vmem=VMEM_BUDGET):
    return pltpu.CompilerParams(dimension_semantics=sem, vmem_limit_bytes=int(vmem))


def _bdot(a, b):
    return jnp.dot(a.astype(BF16), b.astype(BF16), preferred_element_type=F32)


def _bdot_nt(a, b):
    return lax.dot_general(a.astype(BF16), b.astype(BF16), (((1,), (1,)), ((), ())), preferred_element_type=F32)


def _bdot_tn(a, b):
    return lax.dot_general(a.astype(BF16), b.astype(BF16), (((0,), (0,)), ((), ())), preferred_element_type=F32)


def _split2(x):
    hi = x.astype(BF16)
    lo = (x - hi.astype(F32)).astype(BF16)
    return hi, lo


def _dot2(x, w_bf16):
    hi, lo = _split2(x)
    return (jnp.dot(hi, w_bf16, preferred_element_type=F32) + jnp.dot(lo, w_bf16, preferred_element_type=F32))


def _sigmoid(x):
    return 1.0 / (1.0 + jnp.exp(-x))


def _silu(x):
    return x * _sigmoid(x)


def _softplus(x):
    return jnp.maximum(x, 0.0) + jnp.log(1.0 + jnp.exp(-jnp.abs(x)))


def _rmsnorm_body(x_ref, g_ref, o_ref):
    x = x_ref[...]
    y = x * lax.rsqrt(jnp.mean(x * x, axis=-1, keepdims=True) + EPS)
    o_ref[...] = (y * g_ref[...]).astype(o_ref.dtype)


def rmsnorm_rows(x, g3, layer, out_dtype, tm):
    m, d = x.shape
    return pl.pallas_call(
        _rmsnorm_body,
        grid=(m // tm,),
        in_specs=[pl.BlockSpec((tm, d), lambda i: (i, 0)),
                  pl.BlockSpec((None, 1, d), lambda i: (layer, 0, 0))],
        out_specs=pl.BlockSpec((tm, d), lambda i: (i, 0)),
        out_shape=jax.ShapeDtypeStruct((m, d), out_dtype),
        compiler_params=_cparams(("parallel",)),
        name="rmsnorm_rows",
    )(x, g3)


def _mm_body(a_ref, w_ref, o_ref):
    o_ref[...] = jnp.dot(a_ref[...], w_ref[...].astype(BF16), preferred_element_type=F32).astype(o_ref.dtype)


def _mm_res_body(a_ref, w_ref, r_ref, o_ref):
    o_ref[...] = r_ref[...] + jnp.dot(a_ref[...], w_ref[...].astype(BF16), preferred_element_type=F32)


def matmul_rows(a, w3, layer, *, tm, tn, n=None, residual=None, name="matmul_rows"):
    m, k = a.shape
    n = w3.shape[2] if n is None else n
    in_specs = [pl.BlockSpec((tm, k), lambda i, j: (i, 0)),
                pl.BlockSpec((None, k, tn), lambda i, j: (layer, 0, j))]
    args = [a, w3]
    body = _mm_body
    if residual is not None:
        in_specs.append(pl.BlockSpec((tm, tn), lambda i, j: (i, j)))
        args.append(residual)
        body = _mm_res_body
    return pl.pallas_call(
        body,
        grid=(m // tm, n // tn),
        in_specs=in_specs,
        out_specs=pl.BlockSpec((tm, tn), lambda i, j: (i, j)),
        out_shape=jax.ShapeDtypeStruct((m, n), F32),
        compiler_params=_cparams(("parallel", "arbitrary")),
        name=name,
    )(*args)


def _rope_tables(pos):
    half = ROPE_DIM // 2
    inv_freq = ROPE_THETA ** (-jnp.arange(half, dtype=F32) / half)
    ang = pos.astype(F32)[:, None] * inv_freq[None, :]
    cos, sin = jnp.cos(ang), jnp.sin(ang)
    n = pos.shape[0]
    ones = jnp.ones((n, HEAD_DIM - ROPE_DIM), F32)
    zeros = jnp.zeros((n, HEAD_DIM - ROPE_DIM), F32)
    c64 = jnp.concatenate([cos, cos, ones], axis=1)
    s64 = jnp.concatenate([-sin, sin, zeros], axis=1)
    return jnp.concatenate([c64, c64], axis=1), jnp.concatenate([s64, s64], axis=1)


def _rope(x, c, s):
    lane = lax.broadcasted_iota(jnp.int32, x.shape, 1) % HEAD_DIM
    partner = jnp.where(lane < ROPE_DIM // 2,
                        pltpu.roll(x, LANES - ROPE_DIM // 2, 1),
                        pltpu.roll(x, ROPE_DIM // 2, 1))
    return x * c + partner * s


def _dup_half(x, g):
    lane = lax.broadcasted_iota(jnp.int32, x.shape, 1)
    sw = pltpu.roll(x, HEAD_DIM, 1)
    if g == 0:
        return jnp.where(lane < HEAD_DIM, x, sw)
    return jnp.where(lane < HEAD_DIM, sw, x)


def _attn_frame(q_rows, cq, sq, kcat, vcat, valid, sink_ref, write):
    r = q_rows.shape[0]
    lane = lax.broadcasted_iota(jnp.int32, (r, LANES), 1)
    lane_v = lax.broadcasted_iota(jnp.int32, (3 * WINDOW, LANES), 1)
    for p in range(N_HEADS // 2):
        g = (2 * p) // (N_HEADS // N_KV)
        qp = _rope(q_rows[:, LANES * p:LANES * (p + 1)], cq, sq)
        acc = jnp.zeros((r, LANES), F32)
        for hh in range(2):
            h = 2 * p + hh
            in_half = (lane >= HEAD_DIM) if hh else (lane < HEAD_DIM)
            qm = jnp.where(in_half, qp, 0.0)
            logits = _bdot_nt(qm, kcat[g]) * (HEAD_DIM ** -0.5)
            logits = jnp.where(valid, logits, NEG_INF)
            sink = sink_ref[h]
            m = jnp.maximum(jnp.max(logits, axis=-1, keepdims=True), sink)
            pr = jnp.exp(logits - m)
            denom = jnp.sum(pr, axis=-1, keepdims=True) + jnp.exp(sink - m)
            v_half = jnp.where((lane_v >= HEAD_DIM) if hh else (lane_v < HEAD_DIM), vcat[g], 0.0)
            acc = acc + _bdot(pr / denom, v_half)
        write(p, acc)


def _frame_masks(r, no_prev):
    qi = lax.broadcasted_iota(jnp.int32, (r, 3 * WINDOW), 0)
    c = lax.broadcasted_iota(jnp.int32, (r, 3 * WINDOW), 1)
    prev_ok = (c < WINDOW) & (c > qi + no_prev * WINDOW)
    cur_ok = (c >= WINDOW) & (c < 2 * WINDOW) & (c - WINDOW <= qi)
    meta_ok = (c >= 2 * WINDOW) & (c < 2 * WINDOW + N_META)
    return prev_ok | cur_ok | meta_ok


def _attn_prompt_body(sink_ref, q_ref, k_ref, v_ref, c_ref, s_ref, o_ref, kr_ref):
    seq = q_ref.shape[0]
    nb = (seq - N_META) // WINDOW
    kr_ref[...] = _rope(k_ref[...], c_ref[...], s_ref[...])
    zpad = jnp.zeros((WINDOW - N_META, LANES), F32)
    kmeta = jnp.concatenate([kr_ref[0:N_META, :], zpad], axis=0)
    vmeta = jnp.concatenate([v_ref[0:N_META, :], zpad], axis=0)
    kmeta2 = [_dup_half(kmeta, g) for g in range(N_KV)]
    vmeta2 = [_dup_half(vmeta, g) for g in range(N_KV)]

    qi = lax.broadcasted_iota(jnp.int32, (N_META, 3 * WINDOW), 0)
    c = lax.broadcasted_iota(jnp.int32, (N_META, 3 * WINDOW), 1)
    valid_meta = (c >= 2 * WINDOW) & (c - 2 * WINDOW <= qi)
    kcat = [jnp.concatenate([kmeta2[g], kmeta2[g], kmeta2[g]], axis=0) for g in range(N_KV)]
    vcat = [jnp.concatenate([vmeta2[g], vmeta2[g], vmeta2[g]], axis=0) for g in range(N_KV)]

    def write_meta(p, val):
        o_ref[0:N_META, LANES * p:LANES * (p + 1)] = val

    _attn_frame(q_ref[0:N_META, :], c_ref[0:N_META, :], s_ref[0:N_META, :], kcat, vcat, valid_meta,
                sink_ref, write_meta)

    def frame(n, carry):
        r0 = pl.multiple_of(N_META + WINDOW * n, SUBLANES)
        p0 = pl.multiple_of(jnp.maximum(r0 - WINDOW, 0), SUBLANES)
        kprev, kcur = kr_ref[pl.ds(p0, WINDOW), :], kr_ref[pl.ds(r0, WINDOW), :]
        vprev, vcur = v_ref[pl.ds(p0, WINDOW), :], v_ref[pl.ds(r0, WINDOW), :]
        kc = [jnp.concatenate([_dup_half(kprev, g), _dup_half(kcur, g), kmeta2[g]], axis=0) for g in range(N_KV)]
        vc = [jnp.concatenate([_dup_half(vprev, g), _dup_half(vcur, g), vmeta2[g]], axis=0) for g in range(N_KV)]
        valid = _frame_masks(WINDOW, jnp.where(n == 0, 1, 0))

        def write(p, val):
            o_ref[pl.ds(r0, WINDOW), LANES * p:LANES * (p + 1)] = val

        _attn_frame(q_ref[pl.ds(r0, WINDOW), :], c_ref[pl.ds(r0, WINDOW), :], s_ref[pl.ds(r0, WINDOW), :],
                    kc, vc, valid, sink_ref, write)
        return carry

    lax.fori_loop(0, nb, frame, 0)


def attn_prompt(z, sinks, cos_t, sin_t, n_batch, seq):
    grid_spec = pltpu.PrefetchScalarGridSpec(
        num_scalar_prefetch=0,
        grid=(n_batch,),
        in_specs=[pl.BlockSpec(memory_space=pltpu.SMEM),
                  pl.BlockSpec((seq, D_MIX), lambda b: (b, Z_Q // D_MIX)),
                  pl.BlockSpec((seq, LANES), lambda b: (b, Z_AK // LANES)),
                  pl.BlockSpec((seq, LANES), lambda b: (b, Z_AV // LANES)),
                  pl.BlockSpec((seq, LANES), lambda b: (0, 0)),
                  pl.BlockSpec((seq, LANES), lambda b: (0, 0))],
        out_specs=[pl.BlockSpec((seq, D_MIX), lambda b: (b, 0)),
                   pl.BlockSpec((seq, LANES), lambda b: (b, 0))],
    )
    return pl.pallas_call(
        _attn_prompt_body,
        grid_spec=grid_spec,
        out_shape=[jax.ShapeDtypeStruct((n_batch * seq, D_MIX), F32),
                   jax.ShapeDtypeStruct((n_batch * seq, LANES), F32)],
        compiler_params=_cparams(("parallel",)),
        name="attn_prompt",
    )(sinks, z, z, z, cos_t, sin_t)


def _attn_sample_body(sink_ref, q_ref, k_ref, v_ref, c_ref, s_ref, mk_ref, mv_ref, wk_ref, wv_ref, o_ref, kr_ref,
                      *, n_seq, t_len):
    kr_ref[...] = _rope(k_ref[...], jnp.concatenate([c_ref[...]] * n_seq, axis=0),
                        jnp.concatenate([s_ref[...]] * n_seq, axis=0))
    zpad_m = jnp.zeros((WINDOW - N_META, LANES), F32)
    zpad_c = jnp.zeros((WINDOW - t_len, LANES), F32)
    valid = _frame_masks(t_len, 0)
    for s in range(n_seq):
        rows = slice(s * t_len, (s + 1) * t_len)
        kmeta = jnp.concatenate([mk_ref[s], zpad_m], axis=0)
        vmeta = jnp.concatenate([mv_ref[s], zpad_m], axis=0)
        kcur = jnp.concatenate([kr_ref[rows, :], zpad_c], axis=0)
        vcur = jnp.concatenate([v_ref[rows, :], zpad_c], axis=0)
        kprev, vprev = wk_ref[s], wv_ref[s]
        kc = [jnp.concatenate([_dup_half(kprev, g), _dup_half(kcur, g), _dup_half(kmeta, g)], axis=0)
              for g in range(N_KV)]
        vc = [jnp.concatenate([_dup_half(vprev, g), _dup_half(vcur, g), _dup_half(vmeta, g)], axis=0)
              for g in range(N_KV)]

        def write(p, val, rows=rows):
            o_ref[rows, LANES * p:LANES * (p + 1)] = val

        _attn_frame(q_ref[rows, :], c_ref[...], s_ref[...], kc, vc, valid, sink_ref, write)


def attn_sample(z, row0, sinks, cos_t, sin_t, meta_k, meta_v, win_k, win_v, n_seq_total, t_len, n_seq=8):
    rows = n_seq * t_len
    blk0 = row0 // rows
    grid_spec = pltpu.PrefetchScalarGridSpec(
        num_scalar_prefetch=0,
        grid=(n_seq_total // n_seq,),
        in_specs=[pl.BlockSpec(memory_space=pltpu.SMEM),
                  pl.BlockSpec((rows, D_MIX), lambda i: (blk0 + i, Z_Q // D_MIX)),
                  pl.BlockSpec((rows, LANES), lambda i: (blk0 + i, Z_AK // LANES)),
                  pl.BlockSpec((rows, LANES), lambda i: (blk0 + i, Z_AV // LANES)),
                  pl.BlockSpec((t_len, LANES), lambda i: (0, 0)),
                  pl.BlockSpec((t_len, LANES), lambda i: (0, 0)),
                  pl.BlockSpec((n_seq, N_META, LANES), lambda i: (i, 0, 0)),
                  pl.BlockSpec((n_seq, N_META, LANES), lambda i: (i, 0, 0)),
                  pl.BlockSpec((n_seq, WINDOW, LANES), lambda i: (i, 0, 0)),
                  pl.BlockSpec((n_seq, WINDOW, LANES), lambda i: (i, 0, 0))],
        out_specs=[pl.BlockSpec((rows, D_MIX), lambda i: (i, 0)),
                   pl.BlockSpec((rows, LANES), lambda i: (i, 0))],
    )
    return pl.pallas_call(
        functools.partial(_attn_sample_body, n_seq=n_seq, t_len=t_len),
        grid_spec=grid_spec,
        out_shape=[jax.ShapeDtypeStruct((n_seq_total * t_len, D_MIX), F32),
                   jax.ShapeDtypeStruct((n_seq_total * t_len, LANES), F32)],
        compiler_params=_cparams(("parallel",)),
        name="attn_sample",
    )(sinks, z, z, z, cos_t, sin_t, meta_k, meta_v, win_k, win_v)


def _gla_body(q_ref, k_ref, v_ref, r_ref, lr_ref, w2_ref, b_ref, norm_ref, h0_ref, o_ref, h_ref, ht_ref):
    seq = q_ref.shape[0]
    cs = GLA_CHUNK
    ht_ref[...] = h0_ref[...].T
    row = lax.broadcasted_iota(jnp.int32, (cs, cs), 0)
    col = lax.broadcasted_iota(jnp.int32, (cs, cs), 1)
    tril = row >= col
    tril_bf = jnp.where(tril, 1.0, 0.0).astype(BF16)

    def chunk(r0, nvalid):
        nload = min(cs, seq)

        def load(ref):
            x = ref[pl.ds(r0, nload), :]
            if nload < cs:
                x = jnp.concatenate([x, jnp.zeros((cs - nload, x.shape[1]), F32)], axis=0)
            return x

        q, k, v, gr, lr = load(q_ref), load(k_ref), load(v_ref), load(r_ref), load(lr_ref)
        pre = _bdot(lr, w2_ref[...]) + b_ref[...]
        la = (jnp.minimum(pre, 0.0) - jnp.log(1.0 + jnp.exp(-jnp.abs(pre)))) / GLA_NORMALIZER
        if nvalid < cs:
            live = lax.broadcasted_iota(jnp.int32, (cs, 1), 0) < nvalid
            la = jnp.where(live, la, 0.0)
            k = jnp.where(live, k, 0.0)
            v = jnp.where(live, v, 0.0)
        la_hi, la_lo = _split2(la)
        bc = (jnp.dot(tril_bf, la_hi, preferred_element_type=F32)
              + jnp.dot(tril_bf, la_lo, preferred_element_type=F32))
        b_last = bc[cs - 1:cs, :]
        q_in = q * jnp.exp(bc) * (GLA_DK ** -0.5)
        k_in = k * jnp.exp(-bc)
        k_out = k * jnp.exp(b_last - bc)
        att = jnp.where(tril, _bdot_nt(q_in, k_in), 0.0)
        ht = ht_ref[...]
        o = _bdot(att, v) + _bdot_nt(q_in, ht)
        ht_ref[...] = ht * jnp.exp(b_last) + _bdot_tn(v, k_out)
        o = o * lax.rsqrt(jnp.mean(o * o, axis=-1, keepdims=True) + EPS) * norm_ref[...]
        o = o * _silu(gr)
        o_ref[pl.ds(r0, nvalid), :] = o[0:nvalid, :]

    if seq < cs:
        chunk(0, seq)
    else:
        lead = seq % cs
        if lead:
            chunk(0, lead)

        def step(c, carry):
            chunk(pl.multiple_of(lead + cs * c, SUBLANES), cs)
            return carry

        lax.fori_loop(0, seq // cs, step, 0)
    h_ref[...] = ht_ref[...].T


def gla(z1, z2, blk0, n_seq, seq, w2p, b3, norm3, layer, h0):
    def zspec(width, col0):
        return pl.BlockSpec((seq, width), lambda s, h: (blk0 + s, col0 // width + h))

    return pl.pallas_call(
        _gla_body,
        grid=(n_seq, GLA_HEADS),
        in_specs=[zspec(GLA_DK, Z_GQ), zspec(GLA_DK, Z_GK), zspec(GLA_DV, Z_GV), zspec(GLA_DV, Z_GR),
                  pl.BlockSpec((seq, LANES), lambda s, h: (blk0 + s, Z_GLR // LANES)),
                  pl.BlockSpec((None, LANES, GLA_DK), lambda s, h: (layer, 0, h)),
                  pl.BlockSpec((None, 1, GLA_DK), lambda s, h: (layer, 0, h)),
                  pl.BlockSpec((None, 1, GLA_DV), lambda s, h: (layer, 0, 0)),
                  pl.BlockSpec((None, None, GLA_DK, GLA_DV), lambda s, h: (s, h, 0, 0))],
        out_specs=[pl.BlockSpec((seq, GLA_DV), lambda s, h: (s, h)),
                   pl.BlockSpec((None, None, GLA_DK, GLA_DV), lambda s, h: (s, h, 0, 0))],
        out_shape=[jax.ShapeDtypeStruct((n_seq * seq, D_MIX), F32),
                   jax.ShapeDtypeStruct((n_seq, GLA_HEADS, GLA_DK, GLA_DV), F32)],
        scratch_shapes=[pltpu.VMEM((GLA_DV, GLA_DK), F32)],
        compiler_params=_cparams(("parallel", "parallel")),
        name="gla",
    )(z1, z1, z1, z1, z2, w2p, b3, norm3, h0)


def _seg_ones(n, seg):
    r = lax.broadcasted_iota(jnp.int32, (n, n), 0) // seg
    c = lax.broadcasted_iota(jnp.int32, (n, n), 1) // seg
    return jnp.where(r == c, 1.0, 0.0).astype(BF16)


def _rw_pre_body(zr_ref, zk_ref, zv_ref, zl_ref, pr_ref, pk_ref, pv_ref, pl_ref,
                 mur_ref, muk_ref, muv_ref, mul_ref, w0_ref, w2_ref, a0_ref, a2_ref, g2_ref,
                 kk_ref, ka_ref, rk_ref,
                 r_out, w_out, k_out, v_out, kk_out, b_out, g_out, bonus_out):
    rows = zr_ref.shape[0]
    first = lax.broadcasted_iota(jnp.int32, (rows, 1), 0) == 0

    def shifted(z_ref, p_ref, mu_ref):
        z = z_ref[...]
        prev = jnp.where(first, p_ref[...], pltpu.roll(z, 1, 0))
        return z + (prev - z) * mu_ref[...]

    r = shifted(zr_ref, pr_ref, mur_ref)
    k = shifted(zk_ref, pk_ref, muk_ref)
    v = shifted(zv_ref, pv_ref, muv_ref)
    lo = shifted(zl_ref, pl_ref, mul_ref)
    zw, za, zg = lo[:, 0:LANES], lo[:, LANES:2 * LANES], lo[:, 2 * LANES:4 * LANES]
    w = -_softplus(-(w0_ref[...] + _bdot(jnp.tanh(zw), w2_ref[...]))) - 0.5
    decay = jnp.exp(-jnp.exp(w))
    a = _sigmoid(a0_ref[...] + _bdot(za, a2_ref[...]))
    g = _bdot(_sigmoid(zg), g2_ref[...])
    ones = _seg_ones(D_MIX, RW_HEAD)
    kkr = k * kk_ref[...]
    nrm = jnp.sqrt(_dot2(kkr * kkr, ones))
    kk = kkr / jnp.maximum(nrm, 1e-12)
    k2 = k * (1.0 + (a - 1.0) * ka_ref[...])
    r_out[...] = r
    w_out[...] = decay
    k_out[...] = k2
    v_out[...] = v
    kk_out[...] = kk
    b_out[...] = kk * a
    g_out[...] = g
    bonus_out[...] = _dot2(r * k2 * rk_ref[...], ones) * v


def rw_pre(z, prev, n_seq, tiles_per_seq, tt, blk0, lp, layer):
    nt = tiles_per_seq

    def zspec(width, col0):
        return pl.BlockSpec((tt, width), lambda s, j: (blk0 + s * nt + j, col0 // width))

    def pspec(width):
        return pl.BlockSpec((None, 1, width), lambda s, j: (s * nt + j, 0, 0))

    def lspec(width):
        return pl.BlockSpec((None, 1, width), lambda s, j: (layer, 0, 0))

    def wspec(rows_, cols_):
        return pl.BlockSpec((None, rows_, cols_), lambda s, j: (layer, 0, 0))

    tm_spec = pl.BlockSpec((tt, D_MIX), lambda s, j: (j, s))
    rm_spec = pl.BlockSpec((tt, D_MIX), lambda s, j: (s * nt + j, 0))
    tm_shape = jax.ShapeDtypeStruct((nt * tt, n_seq * D_MIX), F32)
    rm_shape = jax.ShapeDtypeStruct((n_seq * nt * tt, D_MIX), F32)
    return pl.pallas_call(
        _rw_pre_body,
        grid=(n_seq, nt),
        in_specs=[zspec(D_MIX, Z_RR), zspec(D_MIX, Z_RK), zspec(D_MIX, Z_RV), zspec(LORA_W, Z_LORA),
                  pspec(D_MIX), pspec(D_MIX), pspec(D_MIX), pspec(LORA_W),
                  lspec(D_MIX), lspec(D_MIX), lspec(D_MIX), lspec(LORA_W),
                  lspec(D_MIX), wspec(LANES, D_MIX), lspec(D_MIX), wspec(LANES, D_MIX), wspec(2 * LANES, D_MIX),
                  lspec(D_MIX), lspec(D_MIX), lspec(D_MIX)],
        out_specs=[tm_spec] * 6 + [rm_spec] * 2,
        out_shape=[tm_shape] * 6 + [rm_shape] * 2,
        compiler_params=_cparams(("parallel", "parallel")),
        name="rw_pre",
    )(z, z, z, z, prev["r"], prev["k"], prev["v"], prev["l"],
      lp["mu_r"], lp["mu_k"], lp["mu_v"], lp["mu_l"], lp["w0"], lp["w2"], lp["a0"], lp["a2"], lp["g2"],
      lp["kk"], lp["ka"], lp["rk"])


RW_GB = 4
RW_TILE = 256
RW_GROUPS = 8


def _rw_scan_body(r_ref, w_ref, k_ref, v_ref, kk_ref, b_ref, s0_ref, y_ref, s_out_ref, s_ref):
    tb = r_ref.shape[0]
    nj = D_MIX // RW_TILE
    hpt = RW_TILE // RW_HEAD

    @pl.when(pl.program_id(1) == 0)
    def _():
        s_ref[...] = s0_ref[...]

    ones4 = _seg_ones(RW_TILE, RW_HEAD)
    rr = lax.broadcasted_iota(jnp.int32, (RW_HEAD, RW_TILE), 0)
    cc = lax.broadcasted_iota(jnp.int32, (RW_HEAD, RW_TILE), 1)
    eye_rep = jnp.where(cc % RW_HEAD == rr, 1.0, 0.0).astype(BF16)
    hr = lax.broadcasted_iota(jnp.int32, (SUBLANES, RW_TILE), 0)
    hc = lax.broadcasted_iota(jnp.int32, (SUBLANES, RW_TILE), 1)
    head_rows = jnp.where(hc // RW_HEAD == hr, 1.0, 0.0)
    tiles = [(bi, j) for bi in range(RW_GB) for j in range(nj)]

    def step(t, carry):
        def row(ref, bi, j):
            return ref[pl.ds(t, 1), D_MIX * bi + RW_TILE * j:D_MIX * bi + RW_TILE * (j + 1)]

        half = len(tiles) // RW_GROUPS
        groups = tuple(tiles[half * gi:half * (gi + 1)] for gi in range(RW_GROUPS))
        s_olds, boths = [], []
        for grp in groups:
            p_list, vd_list, s_old = [], [], []
            for bi, j in grp:
                s = s_ref[bi, :, RW_TILE * j:RW_TILE * (j + 1)]
                s_old.append(s)
                p_list.append((s * row(kk_ref, bi, j)).astype(BF16))
                vd_list.append(eye_rep * row(v_ref, bi, j).astype(BF16))
            s_olds.append(s_old)
            boths.append(jnp.dot(jnp.concatenate(p_list + vd_list, axis=0), ones4, preferred_element_type=F32))
        n = half * RW_HEAD
        for grp, s_old, both in zip(groups, s_olds, boths):
            for idx, (bi, j) in enumerate(grp):
                lanes = slice(RW_TILE * j, RW_TILE * (j + 1))
                sa = both[RW_HEAD * idx:RW_HEAD * (idx + 1)]
                vcol = both[n + RW_HEAD * idx:n + RW_HEAD * (idx + 1)]
                s_new = s_old[idx] * row(w_ref, bi, j) - sa * row(b_ref, bi, j) + vcol * row(k_ref, bi, j)
                s_ref[bi, :, lanes] = s_new
                y8 = lax.dot_general((head_rows * row(r_ref, bi, j)).astype(BF16), s_new.astype(BF16),
                                     (((1,), (1,)), ((), ())), preferred_element_type=F32)
                y_ref[t, RW_HEADS * bi + hpt * j:RW_HEADS * bi + hpt * (j + 1), :] = y8[0:hpt, :]
        return carry

    lax.fori_loop(0, tb, step, 0, unroll=4)

    @pl.when(pl.program_id(1) == pl.num_programs(1) - 1)
    def _():
        s_out_ref[...] = s_ref[...]


def rw_scan(seqs, s0, tb):
    steps, width = seqs[0].shape
    n_seq = width // D_MIX
    gw = RW_GB * D_MIX
    in_spec = pl.BlockSpec((tb, gw), lambda g, i: (i, g))
    st_spec = pl.BlockSpec((RW_GB, RW_HEAD, D_MIX), lambda g, i: (g, 0, 0))
    return pl.pallas_call(
        _rw_scan_body,
        grid=(n_seq // RW_GB, steps // tb),
        in_specs=[in_spec] * 6 + [st_spec],
        out_specs=[pl.BlockSpec((tb, RW_GB * RW_HEADS, RW_HEAD), lambda g, i: (i, g, 0)), st_spec],
        out_shape=[jax.ShapeDtypeStruct((steps, n_seq * RW_HEADS, RW_HEAD), F32),
                   jax.ShapeDtypeStruct((n_seq, RW_HEAD, D_MIX), F32)],
        scratch_shapes=[pltpu.VMEM((RW_GB, RW_HEAD, D_MIX), F32)],
        compiler_params=_cparams(("parallel", "arbitrary")),
        name="rw_scan",
    )(*seqs, s0)


def _rw_post_body(y_ref, g_ref, bonus_ref, lng_ref, lnb_ref, o_ref):
    y = y_ref[...]
    ones = _seg_ones(D_MIX, RW_HEAD)
    mu = _dot2(y, ones) * (1.0 / RW_HEAD)
    d = y - mu
    var = _dot2(d * d, ones) * (1.0 / RW_HEAD)
    yn = d * lax.rsqrt(var + RW_LN_EPS) * lng_ref[...] + lnb_ref[...]
    o_ref[...] = (yn + bonus_ref[...]) * g_ref[...]


def rw_post(y_tm, g, bonus, n_seq, tiles_per_seq, tt, lng3, lnb3, layer):
    nt = tiles_per_seq
    rm_spec = pl.BlockSpec((tt, D_MIX), lambda s, j: (s * nt + j, 0))
    lspec = pl.BlockSpec((None, 1, D_MIX), lambda s, j: (layer, 0, 0))
    return pl.pallas_call(
        _rw_post_body,
        grid=(n_seq, nt),
        in_specs=[pl.BlockSpec((tt, D_MIX), lambda s, j: (j, s)), rm_spec, rm_spec, lspec, lspec],
        out_specs=rm_spec,
        out_shape=jax.ShapeDtypeStruct((n_seq * nt * tt, D_MIX), F32),
        compiler_params=_cparams(("parallel", "parallel")),
        name="rw_post",
    )(y_tm, g, bonus, lng3, lnb3)


S5_W = S5_GROUPS * S5_STATE
S5_KT = 256
S5_NT = S5_KT // S5_GROUP * S5_STATE


def _s5_body(u_ref, bbr_ref, bbi_ref, ccr_ref, cci_ref, d_ref, a_ref, apow_ref, gw_ref, gb_ref, h0r_ref, h0i_ref,
             o_ref, hr_out, hi_out, xr_ref, xi_ref, cr_ref, ci_ref):
    tt = u_ref.shape[0]
    nkt = D_MIX // S5_KT

    @pl.when(pl.program_id(1) == 0)
    def _():
        cr_ref[...] = h0r_ref[...]
        ci_ref[...] = h0i_ref[...]

    u = u_ref[...]
    for jt in range(nkt):
        ub = u[:, S5_KT * jt:S5_KT * (jt + 1)]
        xr_ref[:, S5_NT * jt:S5_NT * (jt + 1)] = _bdot(ub, bbr_ref[jt])
        xi_ref[:, S5_NT * jt:S5_NT * (jt + 1)] = _bdot(ub, bbi_ref[jt])

    sub = lax.broadcasted_iota(jnp.int32, (SUBLANES, S5_W), 0)
    a_re = [a_ref[2 * i:2 * i + 1, :] for i in range(3)]
    a_im = [a_ref[2 * i + 1:2 * i + 2, :] for i in range(3)]
    pw_re, pw_im = apow_ref[0], apow_ref[1]

    def group(gi, carry):
        r0 = pl.multiple_of(gi * SUBLANES, SUBLANES)
        hr, hi = xr_ref[pl.ds(r0, SUBLANES), :], xi_ref[pl.ds(r0, SUBLANES), :]
        for i, sh in enumerate((1, 2, 4)):
            keep = sub >= sh
            sr = jnp.where(keep, pltpu.roll(hr, sh, 0), 0.0)
            si = jnp.where(keep, pltpu.roll(hi, sh, 0), 0.0)
            hr, hi = hr + a_re[i] * sr - a_im[i] * si, hi + a_re[i] * si + a_im[i] * sr
        c_r, c_i = cr_ref[...], ci_ref[...]
        hr = hr + pw_re * c_r - pw_im * c_i
        hi = hi + pw_re * c_i + pw_im * c_r
        xr_ref[pl.ds(r0, SUBLANES), :] = hr
        xi_ref[pl.ds(r0, SUBLANES), :] = hi
        cr_ref[...] = hr[SUBLANES - 1:SUBLANES, :]
        ci_ref[...] = hi[SUBLANES - 1:SUBLANES, :]
        return carry

    lax.fori_loop(0, tt // SUBLANES, group, 0)

    ys = []
    for jt in range(nkt):
        hr = xr_ref[:, S5_NT * jt:S5_NT * (jt + 1)]
        hi = xi_ref[:, S5_NT * jt:S5_NT * (jt + 1)]
        ys.append(_bdot(hr, ccr_ref[jt]) - _bdot(hi, cci_ref[jt]))
    y = jnp.concatenate(ys, axis=1) + d_ref[...] * u
    ya = 0.5 * y * (1.0 + jnp.tanh(math.sqrt(2.0 / math.pi) * (y + 0.044715 * (y * y * y))))
    o_ref[...] = ya * _sigmoid(_bdot(ya, gw_ref[...]) + gb_ref[...])

    @pl.when(pl.program_id(1) == pl.num_programs(1) - 1)
    def _():
        hr_out[...] = cr_ref[...]
        hi_out[...] = ci_ref[...]


def s5(z, blk0, n_seq, tiles_per_seq, tt, sp, layer, h0r, h0i):
    nt = tiles_per_seq
    nkt = D_MIX // S5_KT

    def const3(shape):
        return pl.BlockSpec(shape, lambda s, j: (0,) * len(shape))

    st_spec = pl.BlockSpec((None, 1, S5_W), lambda s, j: (s, 0, 0))
    return pl.pallas_call(
        _s5_body,
        grid=(n_seq, nt),
        in_specs=[pl.BlockSpec((tt, D_MIX), lambda s, j: (blk0 + s * nt + j, Z_S5 // D_MIX)),
                  const3((nkt, S5_KT, S5_NT)), const3((nkt, S5_KT, S5_NT)),
                  const3((nkt, S5_NT, S5_KT)), const3((nkt, S5_NT, S5_KT)),
                  const3((1, D_MIX)), const3((6, S5_W)), const3((2, SUBLANES, S5_W)),
                  pl.BlockSpec((None, D_MIX, D_MIX), lambda s, j: (layer, 0, 0)),
                  pl.BlockSpec((None, 1, D_MIX), lambda s, j: (layer, 0, 0)),
                  st_spec, st_spec],
        out_specs=[pl.BlockSpec((tt, D_MIX), lambda s, j: (s * nt + j, 0)), st_spec, st_spec],
        out_shape=[jax.ShapeDtypeStruct((n_seq * nt * tt, D_MIX), F32),
                   jax.ShapeDtypeStruct((n_seq, 1, S5_W), F32),
                   jax.ShapeDtypeStruct((n_seq, 1, S5_W), F32)],
        scratch_shapes=[pltpu.VMEM((tt, S5_W), F32), pltpu.VMEM((tt, S5_W), F32),
                        pltpu.VMEM((1, S5_W), F32), pltpu.VMEM((1, S5_W), F32)],
        compiler_params=_cparams(("parallel", "arbitrary")),
        name="s5",
    )(z, sp["bbr"], sp["bbi"], sp["ccr"], sp["cci"], sp["d"], sp["a"], sp["apow"], sp["glu_w"], sp["glu_b"], h0r, h0i)


def _s5_tables(lam_re, lam_im, log_step, b_re, b_im, c_re, c_im, d):
    dt = jnp.exp(log_step.astype(F32))[:, None]
    lr = jnp.minimum(lam_re.astype(F32), -1e-4)
    li = lam_im.astype(F32)
    mag = jnp.exp(lr * dt)
    ab_re, ab_im = mag * jnp.cos(li * dt), mag * jnp.sin(li * dt)
    den = lr * lr + li * li
    f_re = ((ab_re - 1.0) * lr + ab_im * li) / den
    f_im = (ab_im * lr - (ab_re - 1.0) * li) / den
    bb_re = f_re[..., None] * b_re - f_im[..., None] * b_im
    bb_im = f_re[..., None] * b_im + f_im[..., None] * b_re
    gpt = S5_KT // S5_GROUP
    nkt = S5_GROUPS // gpt
    eye = jnp.eye(gpt, dtype=F32)

    def in_tiles(bb):
        t = bb.reshape(nkt, gpt, S5_STATE, S5_GROUP)
        t = jnp.einsum('jgph,gk->jghkp', t, eye)
        return t.reshape(nkt, S5_KT, S5_NT)

    def out_tiles(cc):
        t = cc.reshape(nkt, gpt, S5_GROUP, S5_STATE)
        t = jnp.einsum('jghp,gk->jgpkh', t, eye)
        return t.reshape(nkt, S5_NT, S5_KT)

    ar, ai = ab_re.reshape(1, S5_W), ab_im.reshape(1, S5_W)
    pows_r, pows_i = [ar], [ai]
    for _ in range(SUBLANES - 1):
        pr, pi = pows_r[-1], pows_i[-1]
        pows_r.append(pr * ar - pi * ai)
        pows_i.append(pr * ai + pi * ar)
    a_tab = jnp.concatenate([pows_r[0], pows_i[0], pows_r[1], pows_i[1], pows_r[3], pows_i[3]], axis=0)
    apow = jnp.stack([jnp.concatenate(pows_r, axis=0), jnp.concatenate(pows_i, axis=0)])
    return {"bbr": in_tiles(bb_re).astype(BF16), "bbi": in_tiles(bb_im).astype(BF16),
            "ccr": out_tiles(c_re.astype(F32)).astype(BF16), "cci": out_tiles(c_im.astype(F32)).astype(BF16), "d": d.reshape(1, D_MIX).astype(F32), "a": a_tab, "apow": apow}


def _merge_body(o0, o1, o2, o3, wb_ref, g0, g1, g2, g3, out_ref):
    acc = None
    for i, (o_ref, g_ref) in enumerate(((o0, g0), (o1, g1), (o2, g2), (o3, g3))):
        term = _sigmoid(g_ref[...]) * _bdot(o_ref[...], wb_ref[i])
        acc = term if acc is None else acc + term
    out_ref[...] = acc.astype(out_ref.dtype)


def merge(outs, w_branch, zgate, layer, *, tm, tn):
    m = outs[0].shape[0]
    nj = D_MODEL // tn
    o_spec = pl.BlockSpec((tm, D_MIX), lambda i, j: (i, 0))

    def gspec(b):
        return pl.BlockSpec((tm, tn), lambda i, j: (i, b * nj + j))

    return pl.pallas_call(
        _merge_body,
        grid=(m // tm, nj),
        in_specs=[o_spec] * 4 + [pl.BlockSpec((None, N_BRANCH, D_MIX, tn), lambda i, j: (layer, 0, 0, j))]
        + [gspec(b) for b in range(N_BRANCH)],
        out_specs=pl.BlockSpec((tm, tn), lambda i, j: (i, j)),
        out_shape=jax.ShapeDtypeStruct((m, D_MODEL), BF16),
        compiler_params=_cparams(("parallel", "arbitrary")),
        name="merge",
    )(*outs, w_branch, zgate, zgate, zgate, zgate)


def _router_body(x_ref, g_ref, wr_ref, br_ref, xn_ref, route_ref):
    x = x_ref[...]
    xn = x * lax.rsqrt(jnp.mean(x * x, axis=-1, keepdims=True) + EPS) * g_ref[...]
    xn_ref[...] = xn
    logits = _bdot(xn, wr_ref[...]) + br_ref[...]
    lane = lax.broadcasted_iota(jnp.int32, logits.shape, 1)
    big = jnp.int32(1 << 20)
    ninf = -jnp.inf
    lg = jnp.where(lane < N_GROUPS, logits, ninf)
    gmax = jnp.max(lg, axis=-1, keepdims=True)
    g_top = jnp.min(jnp.where(lg == gmax, lane, big), axis=-1, keepdims=True)
    pg = 1.0 / jnp.sum(jnp.exp(lg - gmax), axis=-1, keepdims=True)
    e_lane = lane - N_GROUPS
    in_group = (e_lane >= 0) & (e_lane < N_EXPERTS) & ((e_lane // EXP_PER_GROUP) == g_top)
    le = jnp.where(in_group, logits, ninf)
    m1 = jnp.max(le, axis=-1, keepdims=True)
    i1 = jnp.min(jnp.where(le == m1, lane, big), axis=-1, keepdims=True)
    le2 = jnp.where(lane == i1, ninf, le)
    m2 = jnp.max(le2, axis=-1, keepdims=True)
    i2 = jnp.min(jnp.where(le2 == m2, lane, big), axis=-1, keepdims=True)
    e2 = jnp.exp(m2 - m1)
    w1 = pg / (1.0 + e2)
    w2 = pg * e2 / (1.0 + e2)
    out = jnp.where(lane == 0, (i1 - N_GROUPS).astype(F32),
                    jnp.where(lane == 1, (i2 - N_GROUPS).astype(F32),
                              jnp.where(lane == 2, w1, jnp.where(lane == 3, w2, 0.0))))
    route_ref[...] = out


def router(x, g3, wr3, br3, layer, tm):
    m, d = x.shape
    return pl.pallas_call(
        _router_body,
        grid=(m // tm,),
        in_specs=[pl.BlockSpec((tm, d), lambda i: (i, 0)),
                  pl.BlockSpec((None, 1, d), lambda i: (layer, 0, 0)),
                  pl.BlockSpec((None, d, LANES), lambda i: (layer, 0, 0)),
                  pl.BlockSpec((None, 1, LANES), lambda i: (layer, 0, 0))],
        out_specs=[pl.BlockSpec((tm, d), lambda i: (i, 0)), pl.BlockSpec((tm, LANES), lambda i: (i, 0))],
        out_shape=[jax.ShapeDtypeStruct((m, d), F32), jax.ShapeDtypeStruct((m, LANES), F32)],
        compiler_params=_cparams(("parallel",)),
        name="moe_router",
    )(x, g3, wr3, br3)


def _gather_rows(src_hbm, idx_ref, base, dst, sem, n):
    for r in range(n):
        pltpu.make_async_copy(src_hbm.at[pl.ds(idx_ref[base + r], 1), :], dst.at[pl.ds(r, 1), :], sem).start()


def _wait_rows(src_hbm, dst, sem, n):
    for r in range(n):
        pltpu.make_async_copy(src_hbm.at[pl.ds(0, 1), :], dst.at[pl.ds(r, 1), :], sem).wait()


MOE_MB = 256


def _moe_up_body(exp_ref, tok_ref, nused_ref, x_hbm, w1_ref, w3_ref, h_ref, xbuf, sems):
    i = pl.program_id(0)
    n_used = nused_ref[0]
    slot = i % 2

    @pl.when(i == 0)
    def _():
        _gather_rows(x_hbm, tok_ref, 0, xbuf.at[0], sems.at[0], MOE_MB)

    @pl.when(i + 1 < n_used)
    def _():
        _gather_rows(x_hbm, tok_ref, (i + 1) * MOE_MB, xbuf.at[1 - slot], sems.at[1 - slot], MOE_MB)

    @pl.when(i < n_used)
    def _():
        _wait_rows(x_hbm, xbuf.at[slot], sems.at[slot], MOE_MB)
        xb = xbuf[slot].astype(BF16)
        a = jnp.dot(xb, w1_ref[...].astype(BF16), preferred_element_type=F32)
        b = jnp.dot(xb, w3_ref[...].astype(BF16), preferred_element_type=F32)
        h_ref[...] = _silu(a) * b

    @pl.when(i >= n_used)
    def _():
        h_ref[...] = jnp.zeros(h_ref.shape, F32)


def moe_up(blk_exp, slot_tok, n_used, xn, w1, w3, layer):
    n_blocks = blk_exp.shape[0]
    d = xn.shape[1]
    grid_spec = pltpu.PrefetchScalarGridSpec(
        num_scalar_prefetch=3,
        grid=(n_blocks,),
        in_specs=[pl.BlockSpec(memory_space=pl.ANY),
                  pl.BlockSpec((None, None, d, D_EXPERT), lambda i, e, t, u: (layer, e[i], 0, 0)),
                  pl.BlockSpec((None, None, d, D_EXPERT), lambda i, e, t, u: (layer, e[i], 0, 0))],
        out_specs=pl.BlockSpec((MOE_MB, D_EXPERT), lambda i, e, t, u: (i, 0)),
        scratch_shapes=[pltpu.VMEM((2, MOE_MB, d), F32), pltpu.SemaphoreType.DMA((2,))],
    )
    return pl.pallas_call(
        _moe_up_body,
        grid_spec=grid_spec,
        out_shape=jax.ShapeDtypeStruct((n_blocks * MOE_MB, D_EXPERT), F32),
        compiler_params=_cparams(("arbitrary",)),
        name="moe_up",
    )(blk_exp, slot_tok, n_used, xn, w1, w3)


def _moe_down_body(exp_ref, nused_ref, h_ref, w2_ref, sw_ref, y_ref):
    i = pl.program_id(0)

    @pl.when(i < nused_ref[0])
    def _():
        y_ref[...] = (jnp.dot(h_ref[...].astype(BF16), w2_ref[...].astype(BF16), preferred_element_type=F32)
                      * sw_ref[...])

    @pl.when(i >= nused_ref[0])
    def _():
        y_ref[...] = jnp.zeros(y_ref.shape, F32)


def moe_down(blk_exp, n_used, h, w2, slot_w, layer):
    n_blocks = blk_exp.shape[0]
    d = w2.shape[3]
    grid_spec = pltpu.PrefetchScalarGridSpec(
        num_scalar_prefetch=2,
        grid=(n_blocks,),
        in_specs=[pl.BlockSpec((MOE_MB, D_EXPERT), lambda i, e, u: (i, 0)),
                  pl.BlockSpec((None, None, D_EXPERT, d), lambda i, e, u: (layer, e[i], 0, 0)),
                  pl.BlockSpec((MOE_MB, 1), lambda i, e, u: (i, 0))],
        out_specs=pl.BlockSpec((MOE_MB, d), lambda i, e, u: (i, 0)),
    )
    return pl.pallas_call(
        _moe_down_body,
        grid_spec=grid_spec,
        out_shape=jax.ShapeDtypeStruct((n_blocks * MOE_MB, d), F32),
        compiler_params=_cparams(("arbitrary",)),
        name="moe_down",
    )(blk_exp, n_used, h, w2, slot_w)


MOE_TC = 64


def _moe_combine_body(slot_ref, x_ref, y_hbm, o_ref, ybuf, sems):
    i = pl.program_id(0)
    nstep = pl.num_programs(0)
    slot = i % 2
    n = TOP_K * MOE_TC

    @pl.when(i == 0)
    def _():
        _gather_rows(y_hbm, slot_ref, 0, ybuf.at[0], sems.at[0], n)

    @pl.when(i + 1 < nstep)
    def _():
        _gather_rows(y_hbm, slot_ref, (i + 1) * n, ybuf.at[1 - slot], sems.at[1 - slot], n)

    _wait_rows(y_hbm, ybuf.at[slot], sems.at[slot], n)
    o_ref[...] = x_ref[...] + (ybuf[slot, 0:MOE_TC, :] + ybuf[slot, MOE_TC:n, :])


def moe_combine(slot_of, x, yb):
    m, d = x.shape
    grid_spec = pltpu.PrefetchScalarGridSpec(
        num_scalar_prefetch=1,
        grid=(m // MOE_TC,),
        in_specs=[pl.BlockSpec((MOE_TC, d), lambda i, s: (i, 0)),
                  pl.BlockSpec(memory_space=pl.ANY)],
        out_specs=pl.BlockSpec((MOE_TC, d), lambda i, s: (i, 0)),
        scratch_shapes=[pltpu.VMEM((2, TOP_K * MOE_TC, d), F32), pltpu.SemaphoreType.DMA((2,))],
    )
    return pl.pallas_call(
        _moe_combine_body,
        grid_spec=grid_spec,
        out_shape=jax.ShapeDtypeStruct((m, d), F32),
        compiler_params=_cparams(("arbitrary",)),
        name="moe_combine",
    )(slot_of, x, yb)


def _moe_plan(route, n_tok):
    e_idx = route[:, 0:TOP_K].astype(jnp.int32)
    w = route[:, TOP_K:2 * TOP_K]
    n_assign = n_tok * TOP_K
    e_flat = e_idx.reshape(n_assign)
    onehot = (e_flat[:, None] == jnp.arange(N_EXPERTS, dtype=jnp.int32)[None, :]).astype(jnp.int32)
    csum = jnp.cumsum(onehot, axis=0)
    counts = csum[-1]
    rank = jnp.sum((csum - onehot) * onehot, axis=1)
    padded = (counts + MOE_MB - 1) // MOE_MB * MOE_MB
    pad_end = jnp.cumsum(padded)
    pad_start = pad_end - padded
    dest = pad_start[e_flat] + rank
    n_blocks = -(-(n_assign + N_EXPERTS * (MOE_MB - 1)) // MOE_MB)
    n_slots = n_blocks * MOE_MB
    tok_flat = jnp.repeat(jnp.arange(n_tok, dtype=jnp.int32), TOP_K)
    packed = jnp.stack([tok_flat.astype(F32), w.reshape(n_assign)], axis=1)
    slots = jnp.zeros((n_slots, 2), F32).at[dest].set(packed)
    slot_tok = slots[:, 0].astype(jnp.int32)
    slot_w = slots[:, 1:2]
    n_used = (pad_end[N_EXPERTS - 1] // MOE_MB).astype(jnp.int32)
    blk = jnp.minimum(jnp.arange(n_blocks, dtype=jnp.int32), n_used - 1)
    blk_exp = jnp.sum((pad_end[None, :] <= (blk * MOE_MB)[:, None]).astype(jnp.int32), axis=1).astype(jnp.int32)
    d2 = dest.reshape(n_tok // MOE_TC, MOE_TC, TOP_K)
    slot_of = jnp.transpose(d2, (0, 2, 1)).reshape(-1)
    return blk_exp, slot_tok, slot_w, slot_of, n_used.reshape(1)


def kernel(x_prompt, x_sample, cache_attn_meta_k, cache_attn_meta_v, cache_attn_win_k, cache_attn_win_v, state_gla, state_rwkv, state_rwkv_shift, state_s5_re, state_s5_im, meta_tokens, norm_mix, w_in, w_gla_gate, b_gla_gate, gla_norm, attn_sinks, rw_mu, rw_w0, rw_w2, rw_a0, rw_a2, rw_g2, rw_kk, rw_ka, rw_rk, rw_ln_g, rw_ln_b, s5_lam_re, s5_lam_im, s5_log_step, s5_b_re, s5_b_im, s5_c_re, s5_c_im, s5_d, s5_glu_w, s5_glu_b, w_branch, w_out, norm_moe, w_router_group, b_router_group, w_router_expert, b_router_expert, w_exp_gate, w_exp_up, w_exp_down, norm_final):
    depth = w_in.shape[0]
    nb, seq_p = x_prompt.shape[0], x_prompt.shape[1]
    db, t_len = x_sample.shape[0], x_sample.shape[1]
    seq = seq_p + N_META
    rows_p = nb * seq
    rows_s = db * t_len
    n_tok = rows_p + rows_s
    wb_len = cache_attn_win_k.shape[2]

    tm_big = n_tok // 7
    tm_mid = n_tok // 14
    tm_small = n_tok // 19
    rw_nt = 6
    rw_tt = seq // rw_nt
    rw_tb = seq // 43
    s5_tt = seq // 6

    pieces = []
    for bq in range(nb):
        pieces += [meta_tokens.astype(F32), x_prompt[bq]]
    x = jnp.concatenate(pieces + [x_sample.reshape(rows_s, D_MODEL)], axis=0)

    cos_p, sin_p = _rope_tables(jnp.arange(seq, dtype=jnp.int32))
    cos_s, sin_s = _rope_tables(PAST_LEN + jnp.arange(t_len, dtype=jnp.int32))

    def r3(a):
        return a.reshape(depth, 1, a.shape[-1]).astype(F32)

    norm_mix3, norm_moe3 = r3(norm_mix), r3(norm_moe)
    b_gla3, gla_norm3 = r3(b_gla_gate), r3(gla_norm)
    w2p = jnp.pad(w_gla_gate, ((0, 0), (0, LANES - GLA_RANK), (0, 0)))

    def lora_cols(a):
        c = 3 * D_MIX
        zw, za, zg = a[..., c:c + 64], a[..., c + 64:c + 128], a[..., c + 128:c + 288]
        pad = lambda t, n: jnp.pad(t, [(0, 0)] * (t.ndim - 1) + [(0, n - t.shape[-1])])
        lo = jnp.concatenate([pad(zw, LANES), pad(za, LANES), pad(zg, 2 * LANES)], axis=-1)
        return a[..., 0:D_MIX], a[..., D_MIX:2 * D_MIX], a[..., 2 * D_MIX:c], lo

    mu_r, mu_k, mu_v, mu_l = lora_cols(rw_mu)
    rw_params = {
        "mu_r": r3(mu_r), "mu_k": r3(mu_k), "mu_v": r3(mu_v), "mu_l": r3(mu_l),
        "w0": r3(rw_w0), "a0": r3(rw_a0), "kk": r3(rw_kk), "ka": r3(rw_ka),
        "rk": r3(rw_rk.reshape(depth, D_MIX)),
        "w2": jnp.pad(rw_w2, ((0, 0), (0, LANES - RW_DECAY_RANK), (0, 0))),
        "a2": jnp.pad(rw_a2, ((0, 0), (0, LANES - RW_A_RANK), (0, 0))),
        "g2": jnp.pad(rw_g2, ((0, 0), (0, 2 * LANES - RW_G_RANK), (0, 0))),
    }
    ln_g3, ln_b3 = r3(rw_ln_g), r3(rw_ln_b)
    glu_b3 = r3(s5_glu_b)
    wr3 = jnp.pad(jnp.concatenate([w_router_group, w_router_expert], axis=2).astype(F32),
                  ((0, 0), (0, 0), (0, LANES - N_GROUPS - N_EXPERTS)))
    br3 = jnp.pad(jnp.concatenate([b_router_group, b_router_expert], axis=1).astype(F32),
                  ((0, 0), (0, LANES - N_GROUPS - N_EXPERTS))).reshape(depth, 1, LANES)

    w_in2d = w_in.reshape(depth * D_MODEL, w_in.shape[2])
    rec = [[] for _ in range(16)]
    zeros_gla =jnp.zeros((nb, GLA_HEADS, GLA_DK, GLA_DV), F32)
    zeros_rw = jnp.zeros((nb, RW_HEAD, D_MIX), F32)
    zeros_s5 = jnp.zeros((nb, 1, S5_W), F32)

    for l in range(depth):
        zcol = lambda n: jnp.zeros((D_MODEL, n), F32)
        wcols = lambda a, n: lax.slice(w_in2d, (l * D_MODEL, a), ((l + 1) * D_MODEL, a + n))
        rw0 = _SRC_RW
        w_mix = jnp.concatenate([
            wcols(_SRC_Q, 1024), wcols(rw0, 3072), wcols(_SRC_S5, 1024), wcols(_SRC_GV, 2048),
            wcols(rw0 + 3072, 64), zcol(64), wcols(rw0 + 3136, 64), zcol(64), wcols(rw0 + 3200, 160), zcol(96),
            wcols(_SRC_GQ, 1024), wcols(_SRC_K, 256),
            wcols(_SRC_GLR, 16), zcol(112), zcol(NZ_MIX - Z_GLR - LANES)], axis=1)[None]
        w_gate = wcols(_SRC_GATE, N_BRANCH * D_MODEL)[None]
        xn = rmsnorm_rows(x, norm_mix3, l, BF16, tm_small)
        z = matmul_rows(xn, w_mix, 0, tm=tm_big, tn=512, name="in_proj_mix")
        z1 = z2 = z
        zgate = matmul_rows(xn, w_gate, 0, tm=tm_big, tn=512, name="in_proj_gate")

        sinks = attn_sinks[l].astype(F32)
        o_att_p, kr_p = attn_prompt(z1, sinks, cos_p, sin_p, nb, seq)
        mk = cache_attn_meta_k[l].reshape(db, N_META, LANES)
        mv = cache_attn_meta_v[l].reshape(db, N_META, LANES)
        wk = cache_attn_win_k[l].reshape(db, wb_len, LANES)
        wv = cache_attn_win_v[l].reshape(db, wb_len, LANES)
        o_att_s, kr_s = attn_sample(z1, rows_p, sinks, cos_s, sin_s, mk, mv, wk, wv, db, t_len)
        def state_rows(a, lo, hi):
            return jnp.stack([a[bq * seq + lo:bq * seq + hi] for bq in range(nb)]).reshape(nb, hi - lo, N_KV, HEAD_DIM)

        v_flat = z1[:, Z_AV:Z_AV + LANES]
        v_s = v_flat[rows_p:].reshape(db, t_len, N_KV, HEAD_DIM)
        k_s = kr_s.reshape(db, t_len, N_KV, HEAD_DIM)
        new_wk = jnp.concatenate([cache_attn_win_k[l].astype(F32), k_s], axis=1)[:, -wb_len:]
        new_wv = jnp.concatenate([cache_attn_win_v[l].astype(F32), v_s], axis=1)[:, -wb_len:]

        o_gla_p, gla_p = gla(z1, z2, 0, nb, seq, w2p, b_gla3, gla_norm3, l, zeros_gla)
        o_gla_s, gla_s = gla(z1, z2, rows_p // t_len, db, t_len, w2p, b_gla3, gla_norm3, l,
                             state_gla[l].astype(F32))

        last_rows = lax.slice(z2, (rw_tt - 1, 0), (rows_p, NZ_MIX), (rw_tt, 1))
        prev_all = jnp.concatenate([jnp.zeros((1, NZ_MIX), F32), last_rows[:-1]], axis=0)
        starts_seq = (jnp.arange(nb * rw_nt, dtype=jnp.int32) % rw_nt == 0)[:, None]
        prev_all = jnp.where(starts_seq, 0.0, prev_all)[:, None, :]

        def prev_cols(col0, width):
            return prev_all[:, :, col0:col0 + width]

        prev_p = {"r": prev_cols(Z_RR, D_MIX), "k": prev_cols(Z_RK, D_MIX), "v": prev_cols(Z_RV, D_MIX),
                  "l": prev_cols(Z_LORA, LORA_W)}
        sr, sk, sv, sl = lora_cols(state_rwkv_shift[l].astype(F32))
        prev_s = {"r": sr[:, None], "k": sk[:, None], "v": sv[:, None], "l": sl[:, None]}

        pre_p = rw_pre(z2, prev_p, nb, rw_nt, rw_tt, 0, rw_params, l)
        pre_s = rw_pre(z2, prev_s, db, 1, t_len, rows_p // t_len, rw_params, l)
        y_p, st_p = rw_scan(pre_p[:6], zeros_rw, rw_tb)
        s0_s = jnp.transpose(state_rwkv[l].astype(F32), (0, 2, 1, 3)).reshape(db, RW_HEAD, D_MIX)
        y_s, st_s = rw_scan(pre_s[:6], s0_s, t_len)
        o_rw_p = rw_post(y_p.reshape(seq, nb * D_MIX), pre_p[6], pre_p[7], nb, rw_nt, rw_tt, ln_g3, ln_b3, l)
        o_rw_s = rw_post(y_s.reshape(t_len, db * D_MIX), pre_s[6], pre_s[7], db, 1, t_len, ln_g3, ln_b3, l)

        def rw_state(st, n):
            return jnp.transpose(st.reshape(n, RW_HEAD, RW_HEADS, RW_HEAD), (0, 2, 1, 3))

        def shift_out(zl):
            lo = zl[:, Z_LORA:Z_LORA + LORA_W]
            return jnp.concatenate([zl[:, Z_RR:Z_RR + 3 * D_MIX], lo[:, 0:64], lo[:, 128:192], lo[:, 256:416]], axis=1)

        sh_p = shift_out(jnp.concatenate([z2[(bq + 1) * seq - 1:(bq + 1) * seq] for bq in range(nb)], axis=0))
        sh_s = shift_out(z2[rows_p:].reshape(db, t_len, NZ_MIX)[:, t_len - 1])

        s5p = _s5_tables(s5_lam_re[l], s5_lam_im[l], s5_log_step[l], s5_b_re[l], s5_b_im[l], s5_c_re[l], s5_c_im[l],
                         s5_d[l])
        s5p["glu_w"] = s5_glu_w
        s5p["glu_b"] = glu_b3
        o_s5_p, re_p, im_p = s5(z2, 0, nb, 6, s5_tt, s5p, l, zeros_s5, zeros_s5)
        o_s5_s, re_s, im_s = s5(z2, rows_p // t_len, db, 1, t_len, s5p, l,
                                state_s5_re[l].astype(F32).reshape(db, 1, S5_W),
                                state_s5_im[l].astype(F32).reshape(db, 1, S5_W))

        outs = [jnp.concatenate([a, b], axis=0).astype(BF16) for a, b in
                ((o_att_p, o_att_s), (o_gla_p, o_gla_s), (o_rw_p, o_rw_s), (o_s5_p, o_s5_s))]
        merged = merge(outs, w_branch, zgate, l, tm=tm_big, tn=256)
        x = matmul_rows(merged, w_out, l, tm=tm_big, tn=512, residual=x, name="out_proj")

        xn2, route = router(x, norm_moe3, wr3, br3, l, tm_small)
        blk_exp, slot_tok, slot_w, slot_of, n_used = _moe_plan(route, n_tok)
        hmid = moe_up(blk_exp, slot_tok, n_used, xn2, w_exp_gate, w_exp_up, l)
        yb = moe_down(blk_exp, n_used, hmid, w_exp_down, slot_w, l)
        x = moe_combine(slot_of, x, yb)

        vals = (state_rows(kr_p, 0, N_META), state_rows(v_flat, 0, N_META),
                state_rows(kr_p, seq - WINDOW, seq), state_rows(v_flat, seq - WINDOW, seq),
                gla_p, rw_state(st_p, nb), sh_p, re_p.reshape(nb, S5_GROUPS, S5_STATE),
                im_p.reshape(nb, S5_GROUPS, S5_STATE),
                new_wk, new_wv, gla_s, rw_state(st_s, db), sh_s,
                re_s.reshape(db, S5_GROUPS, S5_STATE), im_s.reshape(db, S5_GROUPS, S5_STATE))
        for i in range(16):
            rec[i].append(vals[i])

    y = rmsnorm_rows(x, norm_final.reshape(1, 1, D_MODEL).astype(F32), 0, F32, tm_small)
    y_prompt = jnp.stack([y[bq * seq + N_META:(bq + 1) * seq] for bq in range(nb)])
    y_sample = y[rows_p:].reshape(db, t_len, D_MODEL)
    return (y_prompt, y_sample, *[jnp.stack(r) for r in rec])
```
